```python
import math
import jax, jax.numpy as jnp
from jax import lax
import numpy as np

D_MODEL = 1024
BATCH = 4
SEQ = 4096
DEPTH = 1
DEC_BATCH = 128
DEC_SEQ = 8
PAST_LEN = 2048
PAGE_SIZE = 128

DIFF_HEADS = 8
DIFF_HEAD_DIM = 64
DIFF_V_DIM = 2 * DIFF_HEAD_DIM
ROT_DIM = DIFF_HEAD_DIM // 4
ROPE_THETA = 500000.0
Q_BLOCK = 128
RET_HEADS = 4
RET_KEY_DIM = 256
RET_VALUE_DIM = 512
RET_THETA = 10000.0
RET_CHUNK = 128
FFN_HIDDEN = ((8 * D_MODEL // 3 + 255) // 256) * 256
NORM_EPS = 1e-5
NEG_INF = -1e30

DIFF_QK_W = DIFF_HEADS * 2 * DIFF_HEAD_DIM
DIFF_V_W = DIFF_HEADS * DIFF_V_DIM
RET_QK_W = RET_HEADS * RET_KEY_DIM
RET_V_W = RET_HEADS * RET_VALUE_DIM
IN_WIDTHS = (DIFF_QK_W, DIFF_QK_W, DIFF_V_W, RET_QK_W, RET_QK_W, RET_V_W, RET_V_W, D_MODEL, D_MODEL)
IN_W = sum(IN_WIDTHS)

kernel_name = "hybrid_diffattn_retention_decode_step"

F32 = jnp.float32


def rms_norm(x, g):
    xf = x.astype(F32)
    y = xf * lax.rsqrt(jnp.mean(xf * xf, axis=-1, keepdims=True) + NORM_EPS)
    return (y * g.astype(F32)).astype(x.dtype)


def rms_plain(xf):
    return xf * lax.rsqrt(jnp.mean(xf * xf, axis=-1, keepdims=True) + NORM_EPS)


def partial_rope(x, pos):
    half = ROT_DIM // 2
    freqs = jnp.exp(-math.log(ROPE_THETA) * jnp.arange(half, dtype=F32) * (2.0 / ROT_DIM))
    ang = pos[:, None] * freqs[None, :]
    cos = jnp.cos(ang)[None, :, None, None, :]
    sin = jnp.sin(ang)[None, :, None, None, :]
    xf = x.astype(F32)
    x1 = xf[..., :half]
    x2 = xf[..., half:ROT_DIM]
    out = jnp.concatenate([x1 * cos - x2 * sin, x2 * cos + x1 * sin, xf[..., ROT_DIM:]], axis=-1)
    return out.astype(x.dtype)


def ret_rotate(x, pos):
    half = RET_KEY_DIM // 2
    angle = jnp.exp(-math.log(RET_THETA) * jnp.linspace(0.0, 1.0, half, dtype=F32))
    ang = pos[:, None] * angle[None, :]
    cos = jnp.cos(ang)[None, :, None, :]
    sin = jnp.sin(ang)[None, :, None, :]
    xf = x.astype(F32).reshape(x.shape[:-1] + (half, 2))
    xe, xo = xf[..., 0], xf[..., 1]
    out = jnp.stack([xe * cos - xo * sin, xo * cos + xe * sin], axis=-1).reshape(x.shape)
    return out.astype(x.dtype)


def project_in(xn, w_in, pos):
    B, T = xn.shape[:2]
    splits = [int(s) for s in np.cumsum(IN_WIDTHS)[:-1]]
    qa, ka, va, qr, kr, vr, gr, gate_a, gate_b = jnp.split(xn @ w_in, splits, axis=-1)
    qa = partial_rope(qa.reshape(B, T, DIFF_HEADS, 2, DIFF_HEAD_DIM), pos)
    ka = partial_rope(ka.reshape(B, T, DIFF_HEADS, 2, DIFF_HEAD_DIM), pos)
    va = va.reshape(B, T, DIFF_HEADS, DIFF_V_DIM)
    qr = ret_rotate(qr.reshape(B, T, RET_HEADS, RET_KEY_DIM), pos)
    kr = ret_rotate(kr.reshape(B, T, RET_HEADS, RET_KEY_DIM), pos) * (RET_KEY_DIM ** -0.5)
    vr = vr.reshape(B, T, RET_HEADS, RET_VALUE_DIM)
    return qa, ka, va, qr, kr, vr, gr, gate_a, gate_b


def diff_scores(q, k, v, mask, lam):
    s = jnp.einsum('bqhcd,bkhcd->bhcqk', q, k).astype(F32) * (DIFF_HEAD_DIM ** -0.5)
    s = jnp.where(mask[None, None, None], s, NEG_INF)
    p = jax.nn.softmax(s, axis=-1)
    w = p[:, :, 0] - lam * p[:, :, 1]
    return jnp.einsum('bhqk,bkhe->bqhe', w.astype(v.dtype), v)


def diff_attn_prompt(q, k, v, lam):
    B, T = q.shape[:2]
    nb = T // Q_BLOCK
    qb = q.reshape(B, nb, Q_BLOCK, DIFF_HEADS, 2, DIFF_HEAD_DIM).swapaxes(0, 1)
    kpos = jnp.arange(T)

    def one_block(args):
        qi, bi = args
        qpos = bi * Q_BLOCK + jnp.arange(Q_BLOCK)
        return diff_scores(qi, k, v, qpos[:, None] >= kpos[None, :], lam)

    ob = lax.map(one_block, (qb, jnp.arange(nb)))
    return ob.swapaxes(0, 1).reshape(B, T, DIFF_HEADS, DIFF_V_DIM)


def diff_attn_sample(q, k_new, v_new, k_cache, v_cache, page_table, lam):
    DB, T = q.shape[:2]
    k_past = k_cache[page_table].reshape(DB, -1, DIFF_HEADS, 2, DIFF_HEAD_DIM)
    v_past = v_cache[page_table].reshape(DB, -1, DIFF_HEADS, DIFF_V_DIM)
    P = k_past.shape[1]
    k_all = jnp.concatenate([k_past, k_new], axis=1)
    v_all = jnp.concatenate([v_past, v_new], axis=1)
    mask = jnp.concatenate([jnp.ones((T, P), bool), jnp.tril(jnp.ones((T, T), bool))], axis=1)
    return diff_scores(q, k_all, v_all, mask, lam)


def ret_chunk(S, q, k, v, log_gamma):
    C = q.shape[1]
    idx = jnp.arange(C, dtype=F32)
    rel = idx[:, None] - idx[None, :]
    decay = jnp.where(rel >= 0, jnp.exp(log_gamma[:, None, None] * jnp.maximum(rel, 0.0)), 0.0)
    qk = jnp.einsum('bihd,bjhd->bhij', q, k) * decay[None]
    o = jnp.einsum('bhij,bjhe->bihe', qk, v)
    q_dec = jnp.exp((idx[:, None] + 1.0) * log_gamma[None, :])
    o = o + jnp.einsum('bihd,bhde->bihe', q * q_dec[None, :, :, None], S)
    k_dec = jnp.exp((C - 1.0 - idx)[:, None] * log_gamma[None, :])
    S_new = jnp.exp(C * log_gamma)[None, :, None, None] * S + jnp.einsum('bjhd,bjhe->bhde', k * k_dec[None, :, :, None], v)
    return S_new, o


def retention_prompt(q, k, v, log_gamma):
    B, T = q.shape[:2]
    nc = T // RET_CHUNK

    def chunks(x):
        return x.astype(F32).reshape((B, nc, RET_CHUNK) + x.shape[2:]).swapaxes(0, 1)

    S0 = jnp.zeros((B, RET_HEADS, RET_KEY_DIM, RET_VALUE_DIM), F32)
    S, o = lax.scan(lambda S, c: ret_chunk(S, c[0], c[1], c[2], log_gamma), S0, (chunks(q), chunks(k), chunks(v)))
    return S, o.swapaxes(0, 1).reshape(B, T, RET_HEADS, RET_VALUE_DIM)


def merge_out(oa, orr, gr, gate_a, gate_b, b_gate, subln_g, lam_init, w_pa, w_pb, w_o):
    B, T = oa.shape[:2]
    dtype = gr.dtype
    a = (rms_plain(oa.astype(F32)) * subln_g.astype(F32) * (1.0 - lam_init)).reshape(B, T, DIFF_V_W).astype(dtype)
    r = rms_plain(orr.astype(F32)).reshape(B, T, RET_V_W).astype(dtype) * jax.nn.silu(gr)
    ba, bb = jnp.split(b_gate, 2)
    m = jax.nn.sigmoid(gate_a + ba) * (a @ w_pa) + jax.nn.sigmoid(gate_b + bb) * (r @ w_pb)
    return m @ w_o


def swiglu(x, w_gu, w_down):
    g, u = jnp.split(x @ w_gu, 2, axis=-1)
    return (jax.nn.silu(g) * u) @ w_down


def setup_inputs(seed: int = 0) -> dict:
    key = jax.random.key(seed)
    ks = jax.random.split(key, 24)
    n_pages = PAST_LEN // PAGE_SIZE
    n_used = DEC_BATCH * n_pages
    n_phys = n_used + n_used // 4
    page_table = jax.random.permutation(ks[0], n_phys)[:n_used].reshape(DEC_BATCH, n_pages).astype(jnp.int32)

    def nrm(k, shape, scale):
        return scale * jax.random.normal(k, shape, F32)

    return {
        "x_prompt": nrm(ks[1], (BATCH, SEQ, D_MODEL), 1.0),
        "x_sample": nrm(ks[2], (DEC_BATCH, DEC_SEQ, D_MODEL), 1.0),
        "cache_k": nrm(ks[3], (DEPTH, n_phys, PAGE_SIZE, DIFF_HEADS, 2, DIFF_HEAD_DIM), 1.0),
        "cache_v": nrm(ks[4], (DEPTH, n_phys, PAGE_SIZE, DIFF_HEADS, DIFF_V_DIM), 1.0),
        "state_ret": nrm(ks[5], (DEPTH, DEC_BATCH, RET_HEADS, RET_KEY_DIM, RET_VALUE_DIM), 0.5),
        "page_table": page_table,
        "norm_mix_g": 1.0 + nrm(ks[6], (DEPTH, D_MODEL), 0.02),
        "w_in": nrm(ks[7], (DEPTH, D_MODEL, IN_W), D_MODEL ** -0.5),
        "b_gate": nrm(ks[8], (DEPTH, 2 * D_MODEL), 0.02),
        "lambda_q1": nrm(ks[9], (DEPTH, DIFF_HEAD_DIM), 0.1),
        "lambda_k1": nrm(ks[10], (DEPTH, DIFF_HEAD_DIM), 0.1),
        "lambda_q2": nrm(ks[11], (DEPTH, DIFF_HEAD_DIM), 0.1),
        "lambda_k2": nrm(ks[12], (DEPTH, DIFF_HEAD_DIM), 0.1),
        "subln_g": 1.0 + nrm(ks[13], (DEPTH, DIFF_V_DIM), 0.02),
        "w_pa": nrm(ks[14], (DEPTH, DIFF_V_W, D_MODEL), DIFF_V_W ** -0.5),
        "w_pb": nrm(ks[15], (DEPTH, RET_V_W, D_MODEL), RET_V_W ** -0.5),
        "w_o": nrm(ks[16], (DEPTH, D_MODEL, D_MODEL), D_MODEL ** -0.5),
        "norm_ffn_g": 1.0 + nrm(ks[17], (DEPTH, D_MODEL), 0.02),
        "w_gu": nrm(ks[18], (DEPTH, D_MODEL, 2 * FFN_HIDDEN), D_MODEL ** -0.5),
        "w_down": nrm(ks[19], (DEPTH, FFN_HIDDEN, D_MODEL), FFN_HIDDEN ** -0.5),
        "norm_final_g": 1.0 + nrm(ks[20], (D_MODEL,), 0.02),
    }


def reference(x_prompt, x_sample, cache_k, cache_v, state_ret, page_table,
              norm_mix_g, w_in, b_gate, lambda_q1, lambda_k1, lambda_q2, lambda_k2,
              subln_g, w_pa, w_pb, w_o, norm_ffn_g, w_gu, w_down, norm_final_g):
    T_p = x_prompt.shape[1]
    T_s = x_sample.shape[1]
    past = page_table.shape[1] * cache_k.shape[2]
    pos_p = jnp.arange(T_p, dtype=F32)
    pos_s = jnp.arange(T_s, dtype=F32) + past
    log_gamma = jnp.log(1.0 - jnp.exp2(-5.0 - jnp.arange(RET_HEADS, dtype=F32)))

    hp, hs = x_prompt, x_sample
    kp_l, vp_l, sp_l, ks_l, vs_l, ss_l = [], [], [], [], [], []
    for l in range(DEPTH):
        lam_init = 0.8 - 0.6 * math.exp(-0.3 * l)
        lam = (jnp.exp(jnp.sum(lambda_q1[l].astype(F32) * lambda_k1[l].astype(F32)))
               - jnp.exp(jnp.sum(lambda_q2[l].astype(F32) * lambda_k2[l].astype(F32))) + lam_init)

        xn = rms_norm(hp, norm_mix_g[l])
        qa, ka, va, qr, kr, vr, gr, g_a, g_b = project_in(xn, w_in[l], pos_p)
        oa = diff_attn_prompt(qa, ka, va, lam)
        s_p, orr = retention_prompt(qr, kr, vr, log_gamma)
        hp = hp + merge_out(oa, orr, gr, g_a, g_b, b_gate[l], subln_g[l], lam_init, w_pa[l], w_pb[l], w_o[l])
        hp = hp + swiglu(rms_norm(hp, norm_ffn_g[l]), w_gu[l], w_down[l])
        kp_l.append(ka)
        vp_l.append(va)
        sp_l.append(s_p.astype(x_prompt.dtype))

        xn = rms_norm(hs, norm_mix_g[l])
        qa, ka, va, qr, kr, vr, gr, g_a, g_b = project_in(xn, w_in[l], pos_s)
        oa = diff_attn_sample(qa, ka, va, cache_k[l], cache_v[l], page_table, lam)
        s_s, orr = ret_chunk(state_ret[l].astype(F32), qr.astype(F32), kr.astype(F32), vr.astype(F32), log_gamma)
        hs = hs + merge_out(oa, orr, gr, g_a, g_b, b_gate[l], subln_g[l], lam_init, w_pa[l], w_pb[l], w_o[l])
        hs = hs + swiglu(rms_norm(hs, norm_ffn_g[l]), w_gu[l], w_down[l])
        ks_l.append(ka)
        vs_l.append(va)
        ss_l.append(s_s.astype(state_ret.dtype))

    y_prompt = rms_norm(hp, norm_final_g)
    y_sample = rms_norm(hs, norm_final_g)
    return (y_prompt, y_sample, jnp.stack(kp_l), jnp.stack(vp_l), jnp.stack(sp_l),
            jnp.stack(ks_l), jnp.stack(vs_l), jnp.stack(ss_l))
```

```python
import functools
import math

import jax
import jax.numpy as jnp
import numpy as np
from jax import lax
from jax.experimental import pallas as pl
from jax.experimental.pallas import tpu as pltpu

F32 = jnp.float32
BF16 = jnp.bfloat16

D_MODEL = 1024
DIFF_HEADS = 8
DIFF_HEAD_DIM = 64
DIFF_V_DIM = 2 * DIFF_HEAD_DIM
ROT_DIM = DIFF_HEAD_DIM // 4
ROPE_THETA = 500000.0
RET_HEADS = 4
RET_KEY_DIM = 256
RET_VALUE_DIM = 512
RET_THETA = 10000.0
RET_CHUNK = 128
FFN_HIDDEN = 2816
NORM_EPS = 1e-5
NEG_INF = -1e30

DIFF_QK_W = DIFF_HEADS * 2 * DIFF_HEAD_DIM
DIFF_V_W = DIFF_HEADS * DIFF_V_DIM
RET_QK_W = RET_HEADS * RET_KEY_DIM
RET_V_W = RET_HEADS * RET_VALUE_DIM
IN_W = 3 * DIFF_QK_W + 2 * RET_QK_W + 2 * RET_V_W + 2 * D_MODEL

LANES = 128
ROPE_TAB_W = 3 * LANES
RET_TAB_W = 3 * RET_KEY_DIM
TAB_W = ROPE_TAB_W + RET_TAB_W
VMEM_LIMIT = 56 * 1024 * 1024


def _lam_init(layer):
    return 0.8 - 0.6 * math.exp(-0.3 * layer)


def _sigmoid(x):
    return 1.0 / (1.0 + jnp.exp(-x))


def _rms(x):
    return x * lax.rsqrt(jnp.mean(x * x, axis=-1, keepdims=True) + NORM_EPS)


def _resident(shape):
    return pl.BlockSpec(shape, lambda *_: (0,) * len(shape), pipeline_mode=pl.Buffered(1))


def _rotation_tables(pos):
    half = ROT_DIM // 2
    freqs = jnp.exp(-math.log(ROPE_THETA) * jnp.arange(half, dtype=F32) * (2.0 / ROT_DIM))
    ang = pos[:, None] * freqs[None, :]
    cos, sin = jnp.cos(ang), jnp.sin(ang)
    lane = np.arange(LANES) % DIFF_HEAD_DIM
    idx = lane % half
    rope_c = jnp.where(lane < ROT_DIM, cos[:, idx], 1.0)
    rope_s1 = jnp.where(lane < half, -sin[:, idx], 0.0)
    rope_s2 = jnp.where((lane >= half) & (lane < ROT_DIM), sin[:, idx], 0.0)

    rhalf = RET_KEY_DIM // 2
    angle = jnp.exp(-math.log(RET_THETA) * jnp.linspace(0.0, 1.0, rhalf, dtype=F32))
    rang = pos[:, None] * angle[None, :]
    rcos, rsin = jnp.cos(rang), jnp.sin(rang)
    lane2 = np.arange(RET_KEY_DIM)
    pair = lane2 // 2
    even = lane2 % 2 == 0
    ret_c = rcos[:, pair]
    ret_s1 = jnp.where(even, -rsin[:, pair], 0.0)
    ret_s2 = jnp.where(~even, rsin[:, pair], 0.0)
    return jnp.concatenate([rope_c, rope_s1, rope_s2, ret_c, ret_s1, ret_s2], axis=1).astype(F32)


def _partial_rope(acc, tab):
    c, s1, s2 = tab[:, 0:LANES], tab[:, LANES:2 * LANES], tab[:, 2 * LANES:3 * LANES]
    half = ROT_DIM // 2
    outs = []
    for g in range(acc.shape[1] // LANES):
        x = acc[:, g * LANES:(g + 1) * LANES]
        outs.append(x * c + pltpu.roll(x, LANES - half, 1) * s1 + pltpu.roll(x, half, 1) * s2)
    return jnp.concatenate(outs, axis=1)


def _ret_rotate(acc, tab):
    per_head = RET_KEY_DIM // LANES
    outs = []
    for g in range(acc.shape[1] // LANES):
        o = (g % per_head) * LANES
        c = tab[:, o:o + LANES]
        s1 = tab[:, RET_KEY_DIM + o:RET_KEY_DIM + o + LANES]
        s2 = tab[:, 2 * RET_KEY_DIM + o:2 * RET_KEY_DIM + o + LANES]
        x = acc[:, g * LANES:(g + 1) * LANES]
        outs.append(x * c + pltpu.roll(x, LANES - 1, 1) * s1 + pltpu.roll(x, 1, 1) * s2)
    return jnp.concatenate(outs, axis=1)


def _proj_kernel(x_ref, g_ref, w_ref, tab_ref, qa_ref, ka_ref, va_ref, qr_ref, kr_ref, vr_ref, gr_ref, ga_ref, gb_ref,
                 xn_ref):
    xn_ref[...] = (_rms(x_ref[...]) * g_ref[...]).astype(BF16)

    def mm(c0, n):
        return jnp.dot(xn_ref[...], w_ref[:, c0:c0 + n], preferred_element_type=F32)

    rope_tab = tab_ref[:, 0:ROPE_TAB_W]
    ret_tab = tab_ref[:, ROPE_TAB_W:TAB_W]
    c = 0
    qa_ref[...] = (_partial_rope(mm(c, DIFF_QK_W), rope_tab) * (DIFF_HEAD_DIM ** -0.5)).astype(qa_ref.dtype)
    c += DIFF_QK_W
    ka_ref[...] = _partial_rope(mm(c, DIFF_QK_W), rope_tab).astype(ka_ref.dtype)
    c += DIFF_QK_W
    va_ref[...] = mm(c, DIFF_V_W).astype(va_ref.dtype)
    c += DIFF_V_W
    qr_ref[...] = _ret_rotate(mm(c, RET_QK_W), ret_tab).astype(qr_ref.dtype)
    c += RET_QK_W
    kr_ref[...] = (_ret_rotate(mm(c, RET_QK_W), ret_tab) * (RET_KEY_DIM ** -0.5)).astype(kr_ref.dtype)
    c += RET_QK_W
    vr_ref[...] = mm(c, RET_V_W).astype(vr_ref.dtype)
    c += RET_V_W
    gr_ref[...] = mm(c, RET_V_W).astype(gr_ref.dtype)
    c += RET_V_W
    ga_ref[...] = mm(c, D_MODEL).astype(ga_ref.dtype)
    c += D_MODEL
    gb_ref[...] = mm(c, D_MODEL).astype(gb_ref.dtype)


def _project(x2d, gain, w_bf16, tab, mxu_dtype, tm):
    rows = x2d.shape[0]
    tab_blocks = tab.shape[0] // tm
    widths = (DIFF_QK_W, DIFF_QK_W, DIFF_V_W, RET_QK_W, RET_QK_W, RET_V_W, RET_V_W, D_MODEL, D_MODEL)
    dtypes = (mxu_dtype, F32, F32, mxu_dtype, mxu_dtype, mxu_dtype, F32, F32, F32)
    return pl.pallas_call(
        _proj_kernel,
        grid=(rows // tm,),
        in_specs=[
            pl.BlockSpec((tm, D_MODEL), lambda i: (i, 0)),
            _resident((1, D_MODEL)),
            _resident((D_MODEL, IN_W)),
            pl.BlockSpec((tm, TAB_W), lambda i: (i % tab_blocks, 0)),
        ],
        out_specs=[pl.BlockSpec((tm, w), lambda i: (i, 0)) for w in widths],
        out_shape=[jax.ShapeDtypeStruct((rows, w), d) for w, d in zip(widths, dtypes)],
        scratch_shapes=[pltpu.VMEM((tm, D_MODEL), BF16)],
        compiler_params=pltpu.CompilerParams(dimension_semantics=("parallel",), vmem_limit_bytes=VMEM_LIMIT),
        name="proj",
    )(x2d, gain.reshape(1, D_MODEL), w_bf16, tab)


def _lambda(lam_ref, lam_init):
    lv = lam_ref[...]
    s1 = jnp.sum(lv[0:1] * lv[1:2], axis=-1, keepdims=True)
    s2 = jnp.sum(lv[2:3] * lv[3:4], axis=-1, keepdims=True)
    return jnp.exp(s1) - jnp.exp(s2) + lam_init


def _split_components(q):
    lane = lax.broadcasted_iota(jnp.int32, q.shape, 1)
    zero = jnp.zeros_like(q)
    return jnp.where(lane < DIFF_HEAD_DIM, q, zero), jnp.where(lane >= DIFF_HEAD_DIM, q, zero)


def _dot_nt(a, b):
    return lax.dot_general(a, b, (((1,), (1,)), ((), ())), preferred_element_type=F32)


def _sub_layer_norm(o, sub_ref, lam_init):
    return _rms(o) * sub_ref[...] * (1.0 - lam_init)


def _attn_prompt_kernel(lam_ref, sub_ref, q_ref, k_ref, v_ref, o_ref, m1, l1, a1, m2, l2, a2, *, blk, lam_init):
    qi = pl.program_id(2)
    ki = pl.program_id(3)

    @pl.when(ki == 0)
    def _():
        for m, l, a in ((m1, l1, a1), (m2, l2, a2)):
            m[...] = jnp.full(m.shape, NEG_INF, F32)
            l[...] = jnp.zeros(l.shape, F32)
            a[...] = jnp.zeros(a.shape, F32)

    def step(on_diagonal):
        k = k_ref[...].astype(BF16)
        v = v_ref[...].astype(BF16)
        for qc, m, l, a in zip(_split_components(q_ref[...]), (m1, m2), (l1, l2), (a1, a2)):
            s = _dot_nt(qc, k)
            if on_diagonal:
                row = lax.broadcasted_iota(jnp.int32, s.shape, 0)
                col = lax.broadcasted_iota(jnp.int32, s.shape, 1)
                s = jnp.where(row >= col, s, NEG_INF)
            m_prev = m[...]
            m_new = jnp.maximum(m_prev, jnp.max(s, axis=1, keepdims=True))
            alpha = jnp.exp(m_prev - m_new)
            p = jnp.exp(s - jnp.concatenate([m_new] * (blk // LANES), axis=1))
            l[...] = alpha * l[...] + jnp.sum(p, axis=1, keepdims=True)
            a[...] = alpha * a[...] + jnp.dot(p.astype(BF16), v, preferred_element_type=F32)
            m[...] = m_new

    @pl.when(ki < qi)
    def _():
        step(False)

    @pl.when(ki == qi)
    def _():
        step(True)

    @pl.when(ki == pl.num_programs(3) - 1)
    def _():
        o = a1[...] / l1[...] - _lambda(lam_ref, lam_init) * (a2[...] / l2[...])
        o_ref[...] = _sub_layer_norm(o, sub_ref, lam_init).astype(o_ref.dtype)


def _attn_prompt(lams, subln, qa, ka, va, batch, seq, lam_init, blk=512):
    nb = seq // blk
    q_map = lambda b, h, qi, ki: (b * nb + qi, h)
    kv_map = lambda b, h, qi, ki: (b * nb + jnp.minimum(ki, qi), h)
    stat = pltpu.VMEM((blk, LANES), F32)
    return pl.pallas_call(
        functools.partial(_attn_prompt_kernel, blk=blk, lam_init=lam_init),
        grid=(batch, DIFF_HEADS, nb, nb),
        in_specs=[
            _resident((4, DIFF_HEAD_DIM)),
            _resident((1, DIFF_V_DIM)),
            pl.BlockSpec((blk, DIFF_V_DIM), q_map),
            pl.BlockSpec((blk, DIFF_V_DIM), kv_map),
            pl.BlockSpec((blk, DIFF_V_DIM), kv_map),
        ],
        out_specs=pl.BlockSpec((blk, DIFF_V_DIM), q_map),
        out_shape=jax.ShapeDtypeStruct((batch * seq, DIFF_V_W), BF16),
        scratch_shapes=[stat, stat, stat, stat, stat, stat],
        compiler_params=pltpu.CompilerParams(
            dimension_semantics=("parallel", "parallel", "parallel", "arbitrary"), vmem_limit_bytes=VMEM_LIMIT),
        name="attn_prompt",
    )(lams, subln, qa, ka, va)


def _attn_sample_kernel(pt_ref, lam_ref, sub_ref, q_ref, kn_ref, vn_ref, *rest, n_pages, page, lam_init):
    del pt_ref
    k_pages = rest[:n_pages]
    v_pages = rest[n_pages:2 * n_pages]
    o_ref = rest[2 * n_pages]
    t = q_ref.shape[0]
    lam = _lambda(lam_ref, lam_init)
    pad = jnp.zeros((page - t, DIFF_V_DIM), BF16)
    row = lax.broadcasted_iota(jnp.int32, (2 * t, page), 0) % t
    col = lax.broadcasted_iota(jnp.int32, (2 * t, page), 1)
    outs = []
    for h in range(DIFF_HEADS):
        hs = slice(h * DIFF_V_DIM, (h + 1) * DIFF_V_DIM)
        q = jnp.concatenate(_split_components(q_ref[:, hs].astype(BF16)), axis=0)
        k_new = jnp.concatenate([kn_ref[:, hs].astype(BF16), pad], axis=0)
        v_new = jnp.concatenate([vn_ref[:, hs].astype(BF16), pad], axis=0)
        s_new = jnp.where(col <= row, _dot_nt(q, k_new), NEG_INF)
        s = jnp.concatenate([_dot_nt(q, kp[:, hs].astype(BF16)) for kp in k_pages] + [s_new], axis=1)
        p = jnp.exp(s - jnp.max(s, axis=1, keepdims=True))
        l = jnp.sum(p, axis=1, keepdims=True)
        p = p.astype(BF16)
        acc = jnp.dot(p[:, n_pages * page:], v_new, preferred_element_type=F32)
        for j, vp in enumerate(v_pages):
            acc += jnp.dot(p[:, j * page:(j + 1) * page], vp[:, hs].astype(BF16), preferred_element_type=F32)
        o = acc / l
        outs.append(_sub_layer_norm(o[:t] - lam * o[t:], sub_ref, lam_init))
    o_ref[...] = jnp.concatenate(outs, axis=1).astype(o_ref.dtype)


def _attn_sample(page_table, lams, subln, qa, ka, va, cache_k, cache_v, dec_seq, lam_init):
    dec_batch, n_pages = page_table.shape
    n_phys, page = cache_k.shape[0], cache_k.shape[1]
    tok_map = lambda b, pt: (b, 0)

    def page_spec(j):
        return pl.BlockSpec((None, page, DIFF_QK_W), lambda b, pt: (pt[b * n_pages + j], 0, 0))

    return pl.pallas_call(
        functools.partial(_attn_sample_kernel, n_pages=n_pages, page=page, lam_init=lam_init),
        grid_spec=pltpu.PrefetchScalarGridSpec(
            num_scalar_prefetch=1,
            grid=(dec_batch,),
            in_specs=[
                pl.BlockSpec((4, DIFF_HEAD_DIM), lambda b, pt: (0, 0)),
                pl.BlockSpec((1, DIFF_V_DIM), lambda b, pt: (0, 0)),
                pl.BlockSpec((dec_seq, DIFF_QK_W), tok_map),
                pl.BlockSpec((dec_seq, DIFF_QK_W), tok_map),
                pl.BlockSpec((dec_seq, DIFF_V_W), tok_map),
            ] + [page_spec(j) for j in range(n_pages)] * 2,
            out_specs=pl.BlockSpec((dec_seq, DIFF_V_W), tok_map),
        ),
        out_shape=jax.ShapeDtypeStruct((dec_batch * dec_seq, DIFF_V_W), F32),
        compiler_params=pltpu.CompilerParams(dimension_semantics=("parallel",), vmem_limit_bytes=VMEM_LIMIT),
        name="attn_sample",
    )(page_table.reshape(-1), lams, subln, qa, ka, va,
      *([cache_k.reshape(n_phys, page, DIFF_QK_W)] * n_pages), *([cache_v.reshape(n_phys, page, DIFF_V_W)] * n_pages))


def _decay_pow(shape, axis, lg, scale, offset):
    idx = lax.broadcasted_iota(jnp.int32, shape, axis).astype(F32)
    return jnp.exp((scale * idx + offset) * lg)


def _ret_chunk(s_prev, q, k, v, lg):
    c = q.shape[0]
    rel = (lax.broadcasted_iota(jnp.int32, (c, c), 0) - lax.broadcasted_iota(jnp.int32, (c, c), 1)).astype(F32)
    decay = jnp.where(rel >= 0, jnp.exp(lg * jnp.maximum(rel, 0.0)), 0.0)
    qk = _dot_nt(q.astype(BF16), k.astype(BF16)) * decay
    o = jnp.dot(qk.astype(BF16), v, preferred_element_type=F32)
    q_dec = _decay_pow(q.shape, 0, lg, 1.0, 1.0)
    o = o + jnp.dot((q * q_dec).astype(BF16), s_prev.astype(BF16), preferred_element_type=F32)
    k_dec = _decay_pow(k.shape, 0, lg, -1.0, c - 1.0)
    kv = jnp.dot((k * k_dec).T.astype(BF16), v, preferred_element_type=F32)
    s_new = _decay_pow((1, s_prev.shape[1]), 1, lg, 0.0, float(c)) * s_prev + kv
    return s_new, o


def _gated_ret_out(o, gr):
    return _rms(o) * (gr * _sigmoid(gr))


def _ret_prompt_kernel(lg_ref, q_ref, k_ref, v_ref, gr_ref, r_ref, state_ref, s_ref):
    c = pl.program_id(2)

    @pl.when(c == 0)
    def _():
        s_ref[...] = jnp.zeros(s_ref.shape, F32)

    lg = lg_ref[pl.program_id(1)]
    s_new, o = _ret_chunk(s_ref[...], q_ref[...].astype(F32), k_ref[...].astype(F32), v_ref[...].astype(BF16), lg)
    s_ref[...] = s_new
    r_ref[...] = _gated_ret_out(o, gr_ref[...]).astype(r_ref.dtype)

    @pl.when(c == pl.num_programs(2) - 1)
    def _():
        state_ref[...] = s_new


def _ret_prompt(log_gamma, qr, kr, vr, gr, batch, seq):
    nc = seq // RET_CHUNK
    tok_map = lambda b, h, c: (b * nc + c, h)
    return pl.pallas_call(
        _ret_prompt_kernel,
        grid=(batch, RET_HEADS, nc),
        in_specs=[
            pl.BlockSpec(memory_space=pltpu.SMEM),
            pl.BlockSpec((RET_CHUNK, RET_KEY_DIM), tok_map),
            pl.BlockSpec((RET_CHUNK, RET_KEY_DIM), tok_map),
            pl.BlockSpec((RET_CHUNK, RET_VALUE_DIM), tok_map),
            pl.BlockSpec((RET_CHUNK, RET_VALUE_DIM), tok_map),
        ],
        out_specs=[
            pl.BlockSpec((RET_CHUNK, RET_VALUE_DIM), tok_map),
            pl.BlockSpec((None, None, RET_KEY_DIM, RET_VALUE_DIM), lambda b, h, c: (b, h, 0, 0)),
        ],
        out_shape=[
            jax.ShapeDtypeStruct((batch * seq, RET_V_W), BF16),
            jax.ShapeDtypeStruct((batch, RET_HEADS, RET_KEY_DIM, RET_VALUE_DIM), F32),
        ],
        scratch_shapes=[pltpu.VMEM((RET_KEY_DIM, RET_VALUE_DIM), F32)],
        compiler_params=pltpu.CompilerParams(
            dimension_semantics=("parallel", "parallel", "arbitrary"), vmem_limit_bytes=VMEM_LIMIT),
        name="ret_prompt",
    )(log_gamma, qr, kr, vr, gr)


def _ret_sample_kernel(lg_ref, s_ref, q_ref, k_ref, v_ref, gr_ref, r_ref, state_ref):
    outs = []
    for h in range(RET_HEADS):
        ks = slice(h * RET_KEY_DIM, (h + 1) * RET_KEY_DIM)
        vs = slice(h * RET_VALUE_DIM, (h + 1) * RET_VALUE_DIM)
        s_new, o = _ret_chunk(s_ref[h], q_ref[:, ks], k_ref[:, ks], v_ref[:, vs].astype(BF16), lg_ref[h])
        state_ref[h] = s_new
        outs.append(_gated_ret_out(o, gr_ref[:, vs]))
    r_ref[...] = jnp.concatenate(outs, axis=1).astype(r_ref.dtype)


def _ret_sample(log_gamma, state, qr, kr, vr, gr, dec_seq):
    dec_batch = state.shape[0]
    tok_map = lambda b: (b, 0)
    state_spec = pl.BlockSpec((None, RET_HEADS, RET_KEY_DIM, RET_VALUE_DIM), lambda b: (b, 0, 0, 0))
    return pl.pallas_call(
        _ret_sample_kernel,
        grid=(dec_batch,),
        in_specs=[
            pl.BlockSpec(memory_space=pltpu.SMEM),
            state_spec,
            pl.BlockSpec((dec_seq, RET_QK_W), tok_map),
            pl.BlockSpec((dec_seq, RET_QK_W), tok_map),
            pl.BlockSpec((dec_seq, RET_V_W), tok_map),
            pl.BlockSpec((dec_seq, RET_V_W), tok_map),
        ],
        out_specs=[pl.BlockSpec((dec_seq, RET_V_W), tok_map), state_spec],
        out_shape=[
            jax.ShapeDtypeStruct((dec_batch * dec_seq, RET_V_W), F32),
            jax.ShapeDtypeStruct(state.shape, F32),
        ],
        compiler_params=pltpu.CompilerParams(dimension_semantics=("parallel",), vmem_limit_bytes=VMEM_LIMIT),
        name="ret_sample",
    )(log_gamma, state, qr, kr, vr, gr)


def _merge_ffn_kernel(h_ref, a_ref, r_ref, ga_ref, gb_ref, bg_ref, wpa_ref, wpb_ref, wo_ref, gffn_ref, wgu_ref,
                      wdn_ref, gfin_ref, y_ref):
    pa = jnp.dot(a_ref[...].astype(BF16), wpa_ref[...], preferred_element_type=F32)
    pb = jnp.dot(r_ref[...].astype(BF16), wpb_ref[...], preferred_element_type=F32)
    m = (_sigmoid(ga_ref[...] + bg_ref[:, :D_MODEL]) * pa + _sigmoid(gb_ref[...] + bg_ref[:, D_MODEL:]) * pb)
    h1 = h_ref[...] + jnp.dot(m.astype(BF16), wo_ref[...], preferred_element_type=F32)
    xn = (_rms(h1) * gffn_ref[...]).astype(BF16)
    gu = jnp.dot(xn, wgu_ref[...], preferred_element_type=F32)
    g, u = gu[:, :FFN_HIDDEN], gu[:, FFN_HIDDEN:]
    hid = (g * _sigmoid(g) * u).astype(BF16)
    h2 = h1 + jnp.dot(hid, wdn_ref[...], preferred_element_type=F32)
    y_ref[...] = _rms(h2) * gfin_ref[...]


def _merge_ffn(h2d, a, r, ga, gb, b_gate, w_pa, w_pb, w_o, g_ffn, w_gu, w_dn, g_fin, tm):
    rows = h2d.shape[0]
    row = lambda w: pl.BlockSpec((tm, w), lambda i: (i, 0))
    return pl.pallas_call(
        _merge_ffn_kernel,
        grid=(rows // tm,),
        in_specs=[
            row(D_MODEL), row(DIFF_V_W), row(RET_V_W), row(D_MODEL), row(D_MODEL),
            _resident((1, 2 * D_MODEL)),
            _resident((DIFF_V_W, D_MODEL)), _resident((RET_V_W, D_MODEL)), _resident((D_MODEL, D_MODEL)),
            _resident((1, D_MODEL)),
            _resident((D_MODEL, 2 * FFN_HIDDEN)), _resident((FFN_HIDDEN, D_MODEL)),
            _resident((1, D_MODEL)),
        ],
        out_specs=row(D_MODEL),
        out_shape=jax.ShapeDtypeStruct((rows, D_MODEL), F32),
        compiler_params=pltpu.CompilerParams(dimension_semantics=("parallel",), vmem_limit_bytes=VMEM_LIMIT),
        name="merge_ffn",
    )(h2d, a, r, ga, gb, b_gate.reshape(1, -1), w_pa, w_pb, w_o, g_ffn.reshape(1, -1), w_gu, w_dn,
      g_fin.reshape(1, -1))


def kernel(x_prompt, x_sample, cache_k, cache_v, state_ret, page_table, norm_mix_g, w_in, b_gate, lambda_q1, lambda_k1,
           lambda_q2, lambda_k2, subln_g, w_pa, w_pb, w_o, norm_ffn_g, w_gu, w_down, norm_final_g):
    batch, seq, _ = x_prompt.shape
    dec_batch, dec_seq, _ = x_sample.shape
    depth = w_in.shape[0]
    assert depth == 1, "the final norm is fused into the (single) layer's FFN kernel"
    past = page_table.shape[1] * cache_k.shape[2]
    tm = 256

    log_gamma = jnp.log(1.0 - jnp.exp2(-5.0 - jnp.arange(RET_HEADS, dtype=F32)))
    tab_p = _rotation_tables(jnp.arange(seq, dtype=F32))
    tab_s = jnp.tile(_rotation_tables(jnp.arange(dec_seq, dtype=F32) + past), (tm // dec_seq, 1))

    l = 0
    lam_init = _lam_init(l)
    lams = jnp.stack([lambda_q1[l], lambda_k1[l], lambda_q2[l], lambda_k2[l]]).astype(F32)
    subln = subln_g[l].reshape(1, DIFF_V_DIM).astype(F32)
    w_in_b = w_in[l].astype(BF16)
    weights = (b_gate[l], w_pa[l].astype(BF16), w_pb[l].astype(BF16), w_o[l].astype(BF16), norm_ffn_g[l],
               w_gu[l].astype(BF16), w_down[l].astype(BF16), norm_final_g)

    hp = x_prompt.reshape(batch * seq, D_MODEL)
    qa, ka_p, va_p, qr, kr, vr, gr, g_a, g_b = _project(hp, norm_mix_g[l], w_in_b, tab_p, BF16, tm)
    a = _attn_prompt(lams, subln, qa, ka_p, va_p, batch, seq, lam_init)
    r, s_p = _ret_prompt(log_gamma, qr, kr, vr, gr, batch, seq)
    y_p = _merge_ffn(hp, a, r, g_a, g_b, *weights, tm)

    hs = x_sample.reshape(dec_batch * dec_seq, D_MODEL)
    qa, ka_s, va_s, qr, kr, vr, gr, g_a, g_b = _project(hs, norm_mix_g[l], w_in_b, tab_s, F32, tm)
    a = _attn_sample(page_table, lams, subln, qa, ka_s, va_s, cache_k[l], cache_v[l], dec_seq, lam_init)
    r, s_s = _ret_sample(log_gamma, state_ret[l], qr, kr, vr, gr, dec_seq)
    y_s = _merge_ffn(hs, a, r, g_a, g_b, *weights, tm)

    return (
        y_p.reshape(batch, seq, D_MODEL),
        y_s.reshape(dec_batch, dec_seq, D_MODEL),
        ka_p.reshape(1, batch, seq, DIFF_HEADS, 2, DIFF_HEAD_DIM),
        va_p.reshape(1, batch, seq, DIFF_HEADS, DIFF_V_DIM),
        s_p[None],
        ka_s.reshape(1, dec_batch, dec_seq, DIFF_HEADS, 2, DIFF_HEAD_DIM),
        va_s.reshape(1, dec_batch, dec_seq, DIFF_HEADS, DIFF_V_DIM),
        s_s[None],
    )
```

```python
import functools
import math

import jax
import jax.numpy as jnp
import numpy as np
from jax import lax
from jax.experimental import pallas as pl
from jax.experimental.pallas import tpu as pltpu

F32 = jnp.float32
BF16 = jnp.bfloat16

D_MODEL = 1024
DIFF_HEADS = 8
DIFF_HEAD_DIM = 64
DIFF_V_DIM = 2 * DIFF_HEAD_DIM
ROT_DIM = DIFF_HEAD_DIM // 4
ROPE_THETA = 500000.0
RET_HEADS = 4
RET_KEY_DIM = 256
RET_VALUE_DIM = 512
RET_THETA = 10000.0
RET_CHUNK = 128
FFN_HIDDEN = 2816
NORM_EPS = 1e-5
NEG_INF = -1e30

DIFF_QK_W = DIFF_HEADS * 2 * DIFF_HEAD_DIM
DIFF_V_W = DIFF_HEADS * DIFF_V_DIM
RET_QK_W = RET_HEADS * RET_KEY_DIM
RET_V_W = RET_HEADS * RET_VALUE_DIM
IN_W = 3 * DIFF_QK_W + 2 * RET_QK_W + 2 * RET_V_W + 2 * D_MODEL

LANES = 128
ROPE_TAB_W = 3 * LANES
RET_TAB_W = 3 * RET_KEY_DIM
TAB_W = ROPE_TAB_W + RET_TAB_W
VMEM_LIMIT = 56 * 1024 * 1024
BF16_SUBLANES = 16
ACC_ROWS = DIFF_V_DIM + BF16_SUBLANES
LOG2E = math.log2(math.e)


def _lam_init(layer):
    return 0.8 - 0.6 * math.exp(-0.3 * layer)


def _sigmoid(x):
    return 1.0 / (1.0 + jnp.exp(-x))


def _rms(x):
    return x * lax.rsqrt(jnp.mean(x * x, axis=-1, keepdims=True) + NORM_EPS)


def _resident(shape):
    return pl.BlockSpec(shape, lambda *_: (0,) * len(shape), pipeline_mode=pl.Buffered(1))


def _params(*semantics):
    return pltpu.CompilerParams(dimension_semantics=semantics, vmem_limit_bytes=VMEM_LIMIT)


def _rotation_tables(pos):
    half = ROT_DIM // 2
    freqs = jnp.exp(-math.log(ROPE_THETA) * jnp.arange(half, dtype=F32) * (2.0 / ROT_DIM))
    ang = pos[:, None] * freqs[None, :]
    cos, sin = jnp.cos(ang), jnp.sin(ang)
    lane = np.arange(LANES) % DIFF_HEAD_DIM
    idx = lane % half
    rope_c = jnp.where(lane < ROT_DIM, cos[:, idx], 1.0)
    rope_s1 = jnp.where(lane < half, -sin[:, idx], 0.0)
    rope_s2 = jnp.where((lane >= half) & (lane < ROT_DIM), sin[:, idx], 0.0)

    rhalf = RET_KEY_DIM // 2
    angle = jnp.exp(-math.log(RET_THETA) * jnp.linspace(0.0, 1.0, rhalf, dtype=F32))
    rang = pos[:, None] * angle[None, :]
    rcos, rsin = jnp.cos(rang), jnp.sin(rang)
    lane2 = np.arange(RET_KEY_DIM)
    pair = lane2 // 2
    even = lane2 % 2 == 0
    ret_c = rcos[:, pair]
    ret_s1 = jnp.where(even, -rsin[:, pair], 0.0)
    ret_s2 = jnp.where(~even, rsin[:, pair], 0.0)
    return jnp.concatenate([rope_c, rope_s1, rope_s2, ret_c, ret_s1, ret_s2], axis=1).astype(F32)


def _partial_rope(acc, tab):
    c, s1, s2 = tab[:, 0:LANES], tab[:, LANES:2 * LANES], tab[:, 2 * LANES:3 * LANES]
    half = ROT_DIM // 2
    outs = []
    for g in range(acc.shape[1] // LANES):
        x = acc[:, g * LANES:(g + 1) * LANES]
        outs.append(x * c + pltpu.roll(x, LANES - half, 1) * s1 + pltpu.roll(x, half, 1) * s2)
    return jnp.concatenate(outs, axis=1)


def _ret_rotate(acc, tab):
    per_head = RET_KEY_DIM // LANES
    outs = []
    for g in range(acc.shape[1] // LANES):
        o = (g % per_head) * LANES
        c = tab[:, o:o + LANES]
        s1 = tab[:, RET_KEY_DIM + o:RET_KEY_DIM + o + LANES]
        s2 = tab[:, 2 * RET_KEY_DIM + o:2 * RET_KEY_DIM + o + LANES]
        x = acc[:, g * LANES:(g + 1) * LANES]
        outs.append(x * c + pltpu.roll(x, LANES - 1, 1) * s1 + pltpu.roll(x, 1, 1) * s2)
    return jnp.concatenate(outs, axis=1)


class _Projector:
    def __init__(self, x_ref, g_ref, w_ref, tab_ref, xn_ref):
        xn_ref[...] = (_rms(x_ref[...]) * g_ref[...]).astype(BF16)
        self._xn_ref, self._w_ref = xn_ref, w_ref
        self._rope_tab = tab_ref[:, 0:ROPE_TAB_W]
        self._ret_tab = tab_ref[:, ROPE_TAB_W:TAB_W]
        self._col = 0

    def _mm(self, n):
        c0, self._col = self._col, self._col + n
        return jnp.dot(self._xn_ref[...], self._w_ref[:, c0:c0 + n], preferred_element_type=F32)

    def qa(self):
        return _partial_rope(self._mm(DIFF_QK_W), self._rope_tab) * (DIFF_HEAD_DIM ** -0.5)

    def ka(self):
        return _partial_rope(self._mm(DIFF_QK_W), self._rope_tab)

    def va(self):
        return self._mm(DIFF_V_W)

    def qr(self):
        return _ret_rotate(self._mm(RET_QK_W), self._ret_tab)

    def kr(self):
        return _ret_rotate(self._mm(RET_QK_W), self._ret_tab) * (RET_KEY_DIM ** -0.5)

    def wide(self):
        return self._mm(RET_V_W)

    def gate(self):
        return self._mm(D_MODEL)


def _proj_sample_kernel(x_ref, g_ref, w_ref, tab_ref, qa_ref, ka_ref, va_ref, qr_ref, kr_ref, vr_ref, gr_ref, ga_ref,
                        gb_ref, xn_ref):
    p = _Projector(x_ref, g_ref, w_ref, tab_ref, xn_ref)
    qa_ref[...] = p.qa()
    ka_ref[...] = p.ka()
    va_ref[...] = p.va()
    qr_ref[...] = p.qr()
    kr_ref[...] = p.kr()
    vr_ref[...] = p.wide()
    gr_ref[...] = p.wide()
    ga_ref[...] = p.gate()
    gb_ref[...] = p.gate()


def _proj_prompt_kernel(x_ref, g_ref, w_ref, tab_ref, qx_ref, kn_ref, kt_ref, vt_ref, vo_ref, qr_ref, kr_ref, vr_ref,
                        gr_ref, ga_ref, gb_ref, xn_ref):
    p = _Projector(x_ref, g_ref, w_ref, tab_ref, xn_ref)
    tm = x_ref.shape[0]
    qt = (p.qa() * LOG2E).T.astype(BF16)
    zeros = jnp.zeros((DIFF_HEAD_DIM, tm), BF16)
    for hc in range(2 * DIFF_HEADS):
        comp = hc % 2
        src = qt[hc * DIFF_HEAD_DIM:(hc + 1) * DIFF_HEAD_DIM]
        lo, hi = (src, zeros) if comp == 0 else (zeros, src)
        qx_ref[hc * DIFF_V_DIM:hc * DIFF_V_DIM + DIFF_HEAD_DIM, :] = lo
        qx_ref[hc * DIFF_V_DIM + DIFF_HEAD_DIM:(hc + 1) * DIFF_V_DIM, :] = hi
    k = p.ka()
    kn_ref[...] = k.astype(BF16)
    kt_ref[...] = k.T
    v = p.va()
    vt_ref[...] = v.T.astype(BF16)
    for h in range(DIFF_HEADS):
        vo_ref[pl.ds(h, tm, stride=DIFF_HEADS), :] = v[:, h * DIFF_V_DIM:(h + 1) * DIFF_V_DIM]
    qr_ref[...] = p.qr().astype(BF16)
    kr_ref[...] = p.kr().astype(BF16)
    vr_ref[...] = p.wide().astype(BF16)
    gr_ref[...] = p.wide()
    ga_ref[...] = p.gate()
    gb_ref[...] = p.gate()


def _proj_in_specs(tm, tab_blocks):
    return [
        pl.BlockSpec((tm, D_MODEL), lambda i: (i, 0)),
        _resident((1, D_MODEL)),
        _resident((D_MODEL, IN_W)),
        pl.BlockSpec((tm, TAB_W), lambda i: (i % tab_blocks, 0)),
    ]


def _project_sample(x2d, gain, w_bf16, tab, tm):
    rows = x2d.shape[0]
    widths = (DIFF_QK_W, DIFF_QK_W, DIFF_V_W, RET_QK_W, RET_QK_W, RET_V_W, RET_V_W, D_MODEL, D_MODEL)
    return pl.pallas_call(
        _proj_sample_kernel,
        grid=(rows // tm,),
        in_specs=_proj_in_specs(tm, tab.shape[0] // tm),
        out_specs=[pl.BlockSpec((tm, w), lambda i: (i, 0)) for w in widths],
        out_shape=[jax.ShapeDtypeStruct((rows, w), F32) for w in widths],
        scratch_shapes=[pltpu.VMEM((tm, D_MODEL), BF16)],
        compiler_params=_params("parallel"),
        name="proj_sample",
    )(x2d, gain.reshape(1, D_MODEL), w_bf16, tab)


def _project_prompt(x2d, gain, w_bf16, tab, batch, seq, tm):
    rows = batch * seq
    per_b = seq // tm
    row = lambda w: pl.BlockSpec((tm, w), lambda i: (i, 0))
    transposed = lambda h: pl.BlockSpec((None, h, tm), lambda i: (i // per_b, 0, i % per_b))
    specs = [
        (transposed(2 * DIFF_QK_W), (batch, 2 * DIFF_QK_W, seq), BF16),
        (row(DIFF_QK_W), (rows, DIFF_QK_W), BF16),
        (transposed(DIFF_QK_W), (batch, DIFF_QK_W, seq), F32),
        (transposed(DIFF_V_W), (batch, DIFF_V_W, seq), BF16),
        (pl.BlockSpec((tm * DIFF_HEADS, DIFF_V_DIM), lambda i: (i, 0)), (rows * DIFF_HEADS, DIFF_V_DIM), F32),
        (row(RET_QK_W), (rows, RET_QK_W), BF16),
        (row(RET_QK_W), (rows, RET_QK_W), BF16),
        (row(RET_V_W), (rows, RET_V_W), BF16),
        (row(RET_V_W), (rows, RET_V_W), F32),
        (row(D_MODEL), (rows, D_MODEL), F32),
        (row(D_MODEL), (rows, D_MODEL), F32),
    ]
    return pl.pallas_call(
        _proj_prompt_kernel,
        grid=(rows // tm,),
        in_specs=_proj_in_specs(tm, tab.shape[0] // tm),
        out_specs=[s for s, _, _ in specs],
        out_shape=[jax.ShapeDtypeStruct(shape, dtype) for _, shape, dtype in specs],
        scratch_shapes=[pltpu.VMEM((tm, D_MODEL), BF16)],
        compiler_params=_params("parallel"),
        name="proj_prompt",
    )(x2d, gain.reshape(1, D_MODEL), w_bf16, tab)


def _lambda(lam_ref, lam_init):
    lv = lam_ref[...]
    s1 = jnp.sum(lv[0:1] * lv[1:2], axis=-1, keepdims=True)
    s2 = jnp.sum(lv[2:3] * lv[3:4], axis=-1, keepdims=True)
    return jnp.exp(s1) - jnp.exp(s2) + lam_init


def _dot_nt(a, b):
    return lax.dot_general(a, b, (((1,), (1,)), ((), ())), preferred_element_type=F32)


def _sub_layer_norm(o, sub_ref, lam_init):
    return _rms(o) * sub_ref[...] * (1.0 - lam_init)


def _attn_prompt_kernel(lam_ref, sub_ref, qx_ref, k_ref, vt_ref, o_ref, m_ref, acc_ref, *, lam_init):
    qi = pl.program_id(1)
    ki = pl.program_id(2)
    n_stat = 2 * DIFF_HEADS
    tk = k_ref.shape[0]

    @pl.when(ki == 0)
    def _():
        m_ref[...] = jnp.full(m_ref.shape, NEG_INF, F32)
        acc_ref[...] = jnp.zeros(acc_ref.shape, F32)

    def scores(i):
        h = i // 2
        k_h = k_ref[:, h * DIFF_V_DIM:(h + 1) * DIFF_V_DIM]
        return jnp.dot(k_h, qx_ref[i * DIFF_V_DIM:(i + 1) * DIFF_V_DIM, :], preferred_element_type=F32)

    def step(on_diagonal):
        ones = jnp.ones((ACC_ROWS - DIFF_V_DIM, tk), BF16)
        st_next = scores(0)
        for i in range(n_stat):
            st = st_next
            if i + 1 < n_stat:
                st_next = scores(i + 1)
            h = i // 2
            vt_h = jnp.concatenate([vt_ref[h * DIFF_V_DIM:(h + 1) * DIFF_V_DIM, :], ones], axis=0)
            if on_diagonal:
                key = lax.broadcasted_iota(jnp.int32, st.shape, 0)
                qry = lax.broadcasted_iota(jnp.int32, st.shape, 1)
                st = jnp.where(key <= qry, st, NEG_INF)
            m_prev = m_ref[i:i + 1, :]
            m_new = jnp.maximum(m_prev, jnp.max(st, axis=0, keepdims=True))
            p = jnp.exp2(st - m_new).astype(BF16)
            acc_ref[i] = jnp.exp2(m_prev - m_new) * acc_ref[i] + jnp.dot(vt_h, p, preferred_element_type=F32)
            m_ref[i:i + 1, :] = m_new

    def normalized(i):
        return acc_ref[i, 0:DIFF_V_DIM, :] / acc_ref[i, DIFF_V_DIM:DIFF_V_DIM + 1, :]

    @pl.when(ki < qi)
    def _():
        step(False)

    @pl.when(ki == qi)
    def _():
        step(True)
        lam = _lambda(lam_ref, lam_init)
        for h in range(DIFF_HEADS):
            ot = normalized(2 * h) - lam * normalized(2 * h + 1)
            o_ref[:, h * DIFF_V_DIM:(h + 1) * DIFF_V_DIM] = _sub_layer_norm(ot.T, sub_ref, lam_init).astype(o_ref.dtype)


def _attn_prompt(lams, subln, qx, kn, vt, batch, seq, lam_init, blk=512):
    nb = seq // blk
    n_stat = 2 * DIFF_HEADS
    return pl.pallas_call(
        functools.partial(_attn_prompt_kernel, lam_init=lam_init),
        grid=(batch, nb, nb),
        in_specs=[
            _resident((4, DIFF_HEAD_DIM)),
            _resident((1, DIFF_V_DIM)),
            pl.BlockSpec((None, 2 * DIFF_QK_W, blk), lambda b, qi, ki: (b, 0, qi)),
            pl.BlockSpec((blk, DIFF_QK_W), lambda b, qi, ki: (b * nb + jnp.minimum(ki, qi), 0)),
            pl.BlockSpec((None, DIFF_V_W, blk), lambda b, qi, ki: (b, 0, jnp.minimum(ki, qi))),
        ],
        out_specs=pl.BlockSpec((blk, DIFF_V_W), lambda b, qi, ki: (b * nb + qi, 0)),
        out_shape=jax.ShapeDtypeStruct((batch * seq, DIFF_V_W), BF16),
        scratch_shapes=[
            pltpu.VMEM((n_stat, blk), F32),
            pltpu.VMEM((n_stat, ACC_ROWS, blk), F32),
        ],
        compiler_params=_params("parallel", "parallel", "arbitrary"),
        name="attn_prompt",
    )(lams, subln, qx, kn, vt)


def _attn_sample_kernel(pt_ref, lam_ref, sub_ref, q_ref, kn_ref, vn_ref, *rest, n_pages, page, lam_init):
    del pt_ref
    kt_pages = rest[:n_pages]
    v_pages = rest[n_pages:2 * n_pages]
    o_ref = rest[2 * n_pages]
    t = q_ref.shape[0]
    lam = _lambda(lam_ref, lam_init)
    pad = jnp.zeros((page - t, DIFF_V_DIM), BF16)
    row = lax.broadcasted_iota(jnp.int32, (2 * t, page), 0) % t
    col = lax.broadcasted_iota(jnp.int32, (2 * t, page), 1)
    lane = lax.broadcasted_iota(jnp.int32, (t, DIFF_V_DIM), 1)
    outs = []
    for h in range(DIFF_HEADS):
        hs = slice(h * DIFF_V_DIM, (h + 1) * DIFF_V_DIM)
        qh = q_ref[:, hs].astype(BF16)
        zero = jnp.zeros_like(qh)
        q = jnp.concatenate([jnp.where(lane < DIFF_HEAD_DIM, qh, zero), jnp.where(lane >= DIFF_HEAD_DIM, qh, zero)],
                            axis=0)
        k_new = jnp.concatenate([kn_ref[:, hs].astype(BF16), pad], axis=0)
        v_new = jnp.concatenate([vn_ref[:, hs].astype(BF16), pad], axis=0)
        s_new = jnp.where(col <= row, _dot_nt(q, k_new), NEG_INF)
        s = jnp.concatenate(
            [jnp.dot(q, kp[hs, :].astype(BF16), preferred_element_type=F32) for kp in kt_pages] + [s_new], axis=1)
        p = jnp.exp(s - jnp.max(s, axis=1, keepdims=True))
        l = jnp.sum(p, axis=1, keepdims=True)
        p = p.astype(BF16)
        acc = jnp.dot(p[:, n_pages * page:], v_new, preferred_element_type=F32)
        for j, vp in enumerate(v_pages):
            v_h = vp[pl.ds(h, page, stride=DIFF_HEADS), :].astype(BF16)
            acc += jnp.dot(p[:, j * page:(j + 1) * page], v_h, preferred_element_type=F32)
        o = acc / l
        outs.append(_sub_layer_norm(o[:t] - lam * o[t:], sub_ref, lam_init))
    o_ref[...] = jnp.concatenate(outs, axis=1).astype(o_ref.dtype)


def _attn_sample(page_table, lams, subln, qa, ka, va, kt_pages, v_pages, dec_seq, lam_init):
    dec_batch, n_pages = page_table.shape
    page = kt_pages.shape[2]
    tok_map = lambda b, pt: (b, 0)

    def page_spec(j):
        return pl.BlockSpec((None, DIFF_QK_W, page), lambda b, pt: (pt[b * n_pages + j], 0, 0))

    return pl.pallas_call(
        functools.partial(_attn_sample_kernel, n_pages=n_pages, page=page, lam_init=lam_init),
        grid_spec=pltpu.PrefetchScalarGridSpec(
            num_scalar_prefetch=1,
            grid=(dec_batch,),
            in_specs=[
                pl.BlockSpec((4, DIFF_HEAD_DIM), lambda b, pt: (0, 0)),
                pl.BlockSpec((1, DIFF_V_DIM), lambda b, pt: (0, 0)),
                pl.BlockSpec((dec_seq, DIFF_QK_W), tok_map),
                pl.BlockSpec((dec_seq, DIFF_QK_W), tok_map),
                pl.BlockSpec((dec_seq, DIFF_V_W), tok_map),
            ] + [page_spec(j) for j in range(n_pages)] * 2,
            out_specs=pl.BlockSpec((dec_seq, DIFF_V_W), tok_map),
        ),
        out_shape=jax.ShapeDtypeStruct((dec_batch * dec_seq, DIFF_V_W), F32),
        compiler_params=_params("parallel"),
        name="attn_sample",
    )(page_table.reshape(-1), lams, subln, qa, ka, va, *([kt_pages] * n_pages), *([v_pages] * n_pages))


def _iota_f32(shape, axis):
    return lax.broadcasted_iota(jnp.int32, shape, axis).astype(F32)


def _ret_decays(c, dk, dv, lg):
    rel = _iota_f32((c, c), 0) - _iota_f32((c, c), 1)
    decay = jnp.where(rel >= 0, jnp.exp(lg * jnp.maximum(rel, 0.0)), 0.0)
    q_dec = jnp.exp((_iota_f32((c, dk), 0) + 1.0) * lg)
    k_dec = jnp.exp((c - 1.0 - _iota_f32((c, dk), 0)) * lg)
    s_dec = jnp.exp(jnp.full((1, dv), float(c), F32) * lg)
    return decay, q_dec, k_dec, s_dec


def _ret_chunk(s_prev, q, k, v, decays):
    decay, q_dec, k_dec, s_dec = decays
    qk = _dot_nt(q.astype(BF16), k.astype(BF16)) * decay
    o = jnp.dot(qk.astype(BF16), v, preferred_element_type=F32)
    o = o + jnp.dot((q * q_dec).astype(BF16), s_prev.astype(BF16), preferred_element_type=F32)
    kv = jnp.dot((k * k_dec).T.astype(BF16), v, preferred_element_type=F32)
    return s_dec * s_prev + kv, o


def _gated_ret_out(o, gr):
    return _rms(o) * (gr * _sigmoid(gr))


def _ret_prompt_kernel(lg_ref, q_ref, k_ref, v_ref, gr_ref, r_ref, state_ref, s_ref):
    c = pl.program_id(1)

    @pl.when(c == 0)
    def _():
        s_ref[...] = jnp.zeros(s_ref.shape, F32)

    decays = _ret_decays(RET_CHUNK, RET_KEY_DIM, RET_VALUE_DIM, lg_ref[pl.program_id(0)])
    for b in range(q_ref.shape[0]):
        s_new, o = _ret_chunk(s_ref[b], q_ref[b].astype(F32), k_ref[b].astype(F32), v_ref[b], decays)
        s_ref[b] = s_new
        r_ref[b] = _gated_ret_out(o, gr_ref[b]).astype(r_ref.dtype)

    @pl.when(c == pl.num_programs(1) - 1)
    def _():
        state_ref[...] = s_ref[...]


def _ret_prompt(log_gamma, qr, kr, vr, gr, batch, seq):
    tok = lambda w: pl.BlockSpec((batch, RET_CHUNK, w), lambda h, c: (0, c, h))
    return pl.pallas_call(
        _ret_prompt_kernel,
        grid=(RET_HEADS, seq // RET_CHUNK),
        in_specs=[pl.BlockSpec(memory_space=pltpu.SMEM), tok(RET_KEY_DIM), tok(RET_KEY_DIM), tok(RET_VALUE_DIM),
                  tok(RET_VALUE_DIM)],
        out_specs=[
            tok(RET_VALUE_DIM),
            pl.BlockSpec((batch, None, RET_KEY_DIM, RET_VALUE_DIM), lambda h, c: (0, h, 0, 0)),
        ],
        out_shape=[
            jax.ShapeDtypeStruct((batch, seq, RET_V_W), BF16),
            jax.ShapeDtypeStruct((batch, RET_HEADS, RET_KEY_DIM, RET_VALUE_DIM), F32),
        ],
        scratch_shapes=[pltpu.VMEM((batch, RET_KEY_DIM, RET_VALUE_DIM), F32)],
        compiler_params=_params("parallel", "arbitrary"),
        name="ret_prompt",
    )(log_gamma, qr, kr, vr, gr)


def _ret_sample_kernel(lg_ref, s_ref, q_ref, k_ref, v_ref, gr_ref, r_ref, state_ref):
    t = q_ref.shape[0]
    outs = []
    for h in range(RET_HEADS):
        ks = slice(h * RET_KEY_DIM, (h + 1) * RET_KEY_DIM)
        vs = slice(h * RET_VALUE_DIM, (h + 1) * RET_VALUE_DIM)
        decays = _ret_decays(t, RET_KEY_DIM, RET_VALUE_DIM, lg_ref[h])
        s_new, o = _ret_chunk(s_ref[h], q_ref[:, ks], k_ref[:, ks], v_ref[:, vs].astype(BF16), decays)
        state_ref[h] = s_new
        outs.append(_gated_ret_out(o, gr_ref[:, vs]))
    r_ref[...] = jnp.concatenate(outs, axis=1).astype(r_ref.dtype)


def _ret_sample(log_gamma, state, qr, kr, vr, gr, dec_seq):
    dec_batch = state.shape[0]
    tok_map = lambda b: (b, 0)
    state_spec = pl.BlockSpec((None, RET_HEADS, RET_KEY_DIM, RET_VALUE_DIM), lambda b: (b, 0, 0, 0))
    return pl.pallas_call(
        _ret_sample_kernel,
        grid=(dec_batch,),
        in_specs=[
            pl.BlockSpec(memory_space=pltpu.SMEM),
            state_spec,
            pl.BlockSpec((dec_seq, RET_QK_W), tok_map),
            pl.BlockSpec((dec_seq, RET_QK_W), tok_map),
            pl.BlockSpec((dec_seq, RET_V_W), tok_map),
            pl.BlockSpec((dec_seq, RET_V_W), tok_map),
        ],
        out_specs=[pl.BlockSpec((dec_seq, RET_V_W), tok_map), state_spec],
        out_shape=[
            jax.ShapeDtypeStruct((dec_batch * dec_seq, RET_V_W), F32),
            jax.ShapeDtypeStruct(state.shape, F32),
        ],
        compiler_params=_params("parallel"),
        name="ret_sample",
    )(log_gamma, state, qr, kr, vr, gr)


def _merge_ffn_kernel(h_ref, a_ref, r_ref, ga_ref, gb_ref, bg_ref, wpa_ref, wpb_ref, wo_ref, gffn_ref, wgu_ref,
                      wdn_ref, gfin_ref, y_ref):
    pa = jnp.dot(a_ref[...].astype(BF16), wpa_ref[...], preferred_element_type=F32)
    pb = jnp.dot(r_ref[...].astype(BF16), wpb_ref[...], preferred_element_type=F32)
    m = (_sigmoid(ga_ref[...] + bg_ref[:, :D_MODEL]) * pa + _sigmoid(gb_ref[...] + bg_ref[:, D_MODEL:]) * pb)
    h1 = h_ref[...] + jnp.dot(m.astype(BF16), wo_ref[...], preferred_element_type=F32)
    xn = (_rms(h1) * gffn_ref[...]).astype(BF16)
    gu = jnp.dot(xn, wgu_ref[...], preferred_element_type=F32)
    g, u = gu[:, :FFN_HIDDEN], gu[:, FFN_HIDDEN:]
    hid = (g * _sigmoid(g) * u).astype(BF16)
    h2 = h1 + jnp.dot(hid, wdn_ref[...], preferred_element_type=F32)
    y_ref[...] = _rms(h2) * gfin_ref[...]


def _merge_ffn(h2d, a, r, ga, gb, b_gate, w_pa, w_pb, w_o, g_ffn, w_gu, w_dn, g_fin, tm):
    rows = h2d.shape[0]
    row = lambda w: pl.BlockSpec((tm, w), lambda i: (i, 0))
    return pl.pallas_call(
        _merge_ffn_kernel,
        grid=(rows // tm,),
        in_specs=[
            row(D_MODEL), row(DIFF_V_W), row(RET_V_W), row(D_MODEL), row(D_MODEL),
            _resident((1, 2 * D_MODEL)),
            _resident((DIFF_V_W, D_MODEL)), _resident((RET_V_W, D_MODEL)), _resident((D_MODEL, D_MODEL)),
            _resident((1, D_MODEL)),
            _resident((D_MODEL, 2 * FFN_HIDDEN)), _resident((FFN_HIDDEN, D_MODEL)),
            _resident((1, D_MODEL)),
        ],
        out_specs=row(D_MODEL),
        out_shape=jax.ShapeDtypeStruct((rows, D_MODEL), F32),
        compiler_params=_params("parallel"),
        name="merge_ffn",
    )(h2d, a, r, ga, gb, b_gate.reshape(1, -1), w_pa, w_pb, w_o, g_ffn.reshape(1, -1), w_gu, w_dn,
      g_fin.reshape(1, -1))


def kernel(x_prompt, x_sample, cache_k, cache_v, state_ret, page_table, norm_mix_g, w_in, b_gate, lambda_q1, lambda_k1,
           lambda_q2, lambda_k2, subln_g, w_pa, w_pb, w_o, norm_ffn_g, w_gu, w_down, norm_final_g):
    batch, seq, _ = x_prompt.shape
    dec_batch, dec_seq, _ = x_sample.shape
    depth, n_phys, page = cache_k.shape[:3]
    assert depth == 1, "the final norm is fused into the (single) layer's FFN kernel"
    past = page_table.shape[1] * page
    tm = 256

    log_gamma = jnp.log(1.0 - jnp.exp2(-5.0 - jnp.arange(RET_HEADS, dtype=F32)))
    tab_p = _rotation_tables(jnp.arange(seq, dtype=F32))
    tab_s = jnp.tile(_rotation_tables(jnp.arange(dec_seq, dtype=F32) + past), (tm // dec_seq, 1))

    l = 0
    lam_init = _lam_init(l)
    lams = jnp.stack([lambda_q1[l], lambda_k1[l], lambda_q2[l], lambda_k2[l]]).astype(F32)
    subln = subln_g[l].reshape(1, DIFF_V_DIM).astype(F32)
    w_in_b = w_in[l].astype(BF16)
    weights = (b_gate[l], w_pa[l].astype(BF16), w_pb[l].astype(BF16), w_o[l].astype(BF16), norm_ffn_g[l],
               w_gu[l].astype(BF16), w_down[l].astype(BF16), norm_final_g)
    kt_pages = jnp.transpose(cache_k[l], (0, 2, 3, 4, 1)).reshape(n_phys, DIFF_QK_W, page)
    v_pages = cache_v[l].reshape(n_phys, page * DIFF_HEADS, DIFF_V_DIM)

    hp = x_prompt.reshape(batch * seq, D_MODEL)
    qx, kn, kt_p, vt, v_p, qr, kr, vr, gr, g_a, g_b = _project_prompt(hp, norm_mix_g[l], w_in_b, tab_p, batch, seq, tm)
    a = _attn_prompt(lams, subln, qx, kn, vt, batch, seq, lam_init)
    as_seq = lambda x: x.reshape(batch, seq, x.shape[-1])
    r, s_p = _ret_prompt(log_gamma, as_seq(qr), as_seq(kr), as_seq(vr), as_seq(gr), batch, seq)
    y_p = _merge_ffn(hp, a, r.reshape(batch * seq, RET_V_W), g_a, g_b, *weights, tm)

    hs = x_sample.reshape(dec_batch * dec_seq, D_MODEL)
    qa, ka_s, va_s, qr, kr, vr, gr, g_a, g_b = _project_sample(hs, norm_mix_g[l], w_in_b, tab_s, tm)
    a = _attn_sample(page_table, lams, subln, qa, ka_s, va_s, kt_pages, v_pages, dec_seq, lam_init)
    r, s_s = _ret_sample(log_gamma, state_ret[l], qr, kr, vr, gr, dec_seq)
    y_s = _merge_ffn(hs, a, r, g_a, g_b, *weights, tm)

    k_p = jnp.transpose(kt_p.reshape(1, batch, DIFF_HEADS, 2, DIFF_HEAD_DIM, seq), (0, 1, 5, 2, 3, 4))
    return (
        y_p.reshape(batch, seq, D_MODEL),
        y_s.reshape(dec_batch, dec_seq, D_MODEL),
        k_p,
        v_p.reshape(1, batch, seq, DIFF_HEADS, DIFF_V_DIM),
        s_p[None],
        ka_s.reshape(1, dec_batch, dec_seq, DIFF_HEADS, 2, DIFF_HEAD_DIM),
        va_s.reshape(1, dec_batch, dec_seq, DIFF_HEADS, DIFF_V_DIM),
        s_s[None],
    )
```

```python
import functools
import math

import jax
import jax.numpy as jnp
import numpy as np
from jax import lax
from jax.experimental import pallas as pl
from jax.experimental.pallas import tpu as pltpu

F32 = jnp.float32
BF16 = jnp.bfloat16

D_MODEL = 1024
DIFF_HEADS = 8
DIFF_HEAD_DIM = 64
DIFF_V_DIM = 2 * DIFF_HEAD_DIM
ROT_DIM = DIFF_HEAD_DIM // 4
ROPE_THETA = 500000.0
RET_HEADS = 4
RET_KEY_DIM = 256
RET_VALUE_DIM = 512
RET_THETA = 10000.0
RET_CHUNK = 128
FFN_HIDDEN = 2816
NORM_EPS = 1e-5
NEG_INF = -1e30

DIFF_QK_W = DIFF_HEADS * 2 * DIFF_HEAD_DIM
DIFF_V_W = DIFF_HEADS * DIFF_V_DIM
RET_QK_W = RET_HEADS * RET_KEY_DIM
RET_V_W = RET_HEADS * RET_VALUE_DIM
IN_W = 3 * DIFF_QK_W + 2 * RET_QK_W + 2 * RET_V_W + 2 * D_MODEL

LANES = 128
ROPE_TAB_W = 3 * LANES
RET_TAB_W = 3 * RET_KEY_DIM
TAB_W = ROPE_TAB_W + RET_TAB_W
VMEM_LIMIT = 56 * 1024 * 1024
BF16_SUBLANES = 16
ACC_ROWS = DIFF_V_DIM + BF16_SUBLANES
LOG2E = math.log2(math.e)


def _lam_init(layer):
    return 0.8 - 0.6 * math.exp(-0.3 * layer)


def _sigmoid(x):
    return 1.0 / (1.0 + jnp.exp(-x))


def _rms(x):
    return x * lax.rsqrt(jnp.mean(x * x, axis=-1, keepdims=True) + NORM_EPS)


def _resident(shape):
    return pl.BlockSpec(shape, lambda *_: (0,) * len(shape), pipeline_mode=pl.Buffered(1))


def _params(*semantics):
    return pltpu.CompilerParams(dimension_semantics=semantics, vmem_limit_bytes=VMEM_LIMIT)


def _rotation_tables(pos):
    half = ROT_DIM // 2
    freqs = jnp.exp(-math.log(ROPE_THETA) * jnp.arange(half, dtype=F32) * (2.0 / ROT_DIM))
    ang = pos[:, None] * freqs[None, :]
    cos, sin = jnp.cos(ang), jnp.sin(ang)
    lane = np.arange(LANES) % DIFF_HEAD_DIM
    idx = lane % half
    rope_c = jnp.where(lane < ROT_DIM, cos[:, idx], 1.0)
    rope_s1 = jnp.where(lane < half, -sin[:, idx], 0.0)
    rope_s2 = jnp.where((lane >= half) & (lane < ROT_DIM), sin[:, idx], 0.0)

    rhalf = RET_KEY_DIM // 2
    angle = jnp.exp(-math.log(RET_THETA) * jnp.linspace(0.0, 1.0, rhalf, dtype=F32))
    rang = pos[:, None] * angle[None, :]
    rcos, rsin = jnp.cos(rang), jnp.sin(rang)
    lane2 = np.arange(RET_KEY_DIM)
    pair = lane2 // 2
    even = lane2 % 2 == 0
    ret_c = rcos[:, pair]
    ret_s1 = jnp.where(even, -rsin[:, pair], 0.0)
    ret_s2 = jnp.where(~even, rsin[:, pair], 0.0)
    return jnp.concatenate([rope_c, rope_s1, rope_s2, ret_c, ret_s1, ret_s2], axis=1).astype(F32)


def _partial_rope(acc, tab):
    c, s1, s2 = tab[:, 0:LANES], tab[:, LANES:2 * LANES], tab[:, 2 * LANES:3 * LANES]
    half = ROT_DIM // 2
    outs = []
    for g in range(acc.shape[1] // LANES):
        x = acc[:, g * LANES:(g + 1) * LANES]
        outs.append(x * c + pltpu.roll(x, LANES - half, 1) * s1 + pltpu.roll(x, half, 1) * s2)
    return jnp.concatenate(outs, axis=1)


def _ret_rotate(acc, tab):
    per_head = RET_KEY_DIM // LANES
    outs = []
    for g in range(acc.shape[1] // LANES):
        o = (g % per_head) * LANES
        c = tab[:, o:o + LANES]
        s1 = tab[:, RET_KEY_DIM + o:RET_KEY_DIM + o + LANES]
        s2 = tab[:, 2 * RET_KEY_DIM + o:2 * RET_KEY_DIM + o + LANES]
        x = acc[:, g * LANES:(g + 1) * LANES]
        outs.append(x * c + pltpu.roll(x, LANES - 1, 1) * s1 + pltpu.roll(x, 1, 1) * s2)
    return jnp.concatenate(outs, axis=1)


class _Projector:
    def __init__(self, x_ref, g_ref, w_ref, tab_ref, xn_ref):
        xn_ref[...] = (_rms(x_ref[...]) * g_ref[...]).astype(BF16)
        self._xn_ref, self._w_ref = xn_ref, w_ref
        self._rope_tab = tab_ref[:, 0:ROPE_TAB_W]
        self._ret_tab = tab_ref[:, ROPE_TAB_W:TAB_W]
        self._col = 0

    def _mm(self, n):
        c0, self._col = self._col, self._col + n
        return jnp.dot(self._xn_ref[...], self._w_ref[:, c0:c0 + n], preferred_element_type=F32)

    def qa(self):
        return _partial_rope(self._mm(DIFF_QK_W), self._rope_tab) * (DIFF_HEAD_DIM ** -0.5)

    def ka(self):
        return _partial_rope(self._mm(DIFF_QK_W), self._rope_tab)

    def va(self):
        return self._mm(DIFF_V_W)

    def qr(self):
        return _ret_rotate(self._mm(RET_QK_W), self._ret_tab)

    def kr(self):
        return _ret_rotate(self._mm(RET_QK_W), self._ret_tab) * (RET_KEY_DIM ** -0.5)

    def wide(self):
        return self._mm(RET_V_W)

    def gate(self):
        return self._mm(D_MODEL)


def _proj_sample_kernel(x_ref, g_ref, w_ref, tab_ref, qa_ref, ka_ref, va_ref, qr_ref, kr_ref, vr_ref, gr_ref, ga_ref,
                        gb_ref, xn_ref):
    p = _Projector(x_ref, g_ref, w_ref, tab_ref, xn_ref)
    qa_ref[...] = p.qa()
    ka_ref[...] = p.ka()
    va_ref[...] = p.va()
    qr_ref[...] = p.qr()
    kr_ref[...] = p.kr()
    vr_ref[...] = p.wide()
    gr_ref[...] = p.wide()
    ga_ref[...] = p.gate()
    gb_ref[...] = p.gate()


def _proj_prompt_kernel(x_ref, g_ref, w_ref, tab_ref, qx_ref, kn_ref, kt_ref, vt_ref, vo_ref, qr_ref, kr_ref, vr_ref,
                        gr_ref, ga_ref, gb_ref, xn_ref):
    p = _Projector(x_ref, g_ref, w_ref, tab_ref, xn_ref)
    tm = x_ref.shape[0]
    qt = (p.qa() * LOG2E).T.astype(BF16)
    zeros = jnp.zeros((DIFF_HEAD_DIM, tm), BF16)
    for hc in range(2 * DIFF_HEADS):
        comp = hc % 2
        src = qt[hc * DIFF_HEAD_DIM:(hc + 1) * DIFF_HEAD_DIM]
        lo, hi = (src, zeros) if comp == 0 else (zeros, src)
        qx_ref[hc * DIFF_V_DIM:hc * DIFF_V_DIM + DIFF_HEAD_DIM, :] = lo
        qx_ref[hc * DIFF_V_DIM + DIFF_HEAD_DIM:(hc + 1) * DIFF_V_DIM, :] = hi
    k = p.ka()
    kn_ref[...] = k.astype(BF16)
    kt_ref[...] = k.T
    v = p.va()
    vt_ref[...] = v.T.astype(BF16)
    for h in range(DIFF_HEADS):
        vo_ref[pl.ds(h, tm, stride=DIFF_HEADS), :] = v[:, h * DIFF_V_DIM:(h + 1) * DIFF_V_DIM]
    qr_ref[...] = p.qr().astype(BF16)
    kr_ref[...] = p.kr().astype(BF16)
    vr_ref[...] = p.wide().astype(BF16)
    gr_ref[...] = p.wide()
    ga_ref[...] = p.gate()
    gb_ref[...] = p.gate()


def _proj_in_specs(tm, tab_blocks):
    return [
        pl.BlockSpec((tm, D_MODEL), lambda i: (i, 0)),
        _resident((1, D_MODEL)),
        _resident((D_MODEL, IN_W)),
        pl.BlockSpec((tm, TAB_W), lambda i: (i % tab_blocks, 0)),
    ]


def _project_sample(x2d, gain, w_bf16, tab, tm):
    rows = x2d.shape[0]
    widths = (DIFF_QK_W, DIFF_QK_W, DIFF_V_W, RET_QK_W, RET_QK_W, RET_V_W, RET_V_W, D_MODEL, D_MODEL)
    return pl.pallas_call(
        _proj_sample_kernel,
        grid=(rows // tm,),
        in_specs=_proj_in_specs(tm, tab.shape[0] // tm),
        out_specs=[pl.BlockSpec((tm, w), lambda i: (i, 0)) for w in widths],
        out_shape=[jax.ShapeDtypeStruct((rows, w), F32) for w in widths],
        scratch_shapes=[pltpu.VMEM((tm, D_MODEL), BF16)],
        compiler_params=_params("parallel"),
        name="proj_sample",
    )(x2d, gain.reshape(1, D_MODEL), w_bf16, tab)


def _project_prompt(x2d, gain, w_bf16, tab, batch, seq, tm):
    rows = batch * seq
    per_b = seq // tm
    row = lambda w: pl.BlockSpec((tm, w), lambda i: (i, 0))
    transposed = lambda h: pl.BlockSpec((None, h, tm), lambda i: (i // per_b, 0, i % per_b))
    specs = [
        (transposed(2 * DIFF_QK_W), (batch, 2 * DIFF_QK_W, seq), BF16),
        (row(DIFF_QK_W), (rows, DIFF_QK_W), BF16),
        (transposed(DIFF_QK_W), (batch, DIFF_QK_W, seq), F32),
        (transposed(DIFF_V_W), (batch, DIFF_V_W, seq), BF16),
        (pl.BlockSpec((tm * DIFF_HEADS, DIFF_V_DIM), lambda i: (i, 0)), (rows * DIFF_HEADS, DIFF_V_DIM), F32),
        (row(RET_QK_W), (rows, RET_QK_W), BF16),
        (row(RET_QK_W), (rows, RET_QK_W), BF16),
        (row(RET_V_W), (rows, RET_V_W), BF16),
        (row(RET_V_W), (rows, RET_V_W), F32),
        (row(D_MODEL), (rows, D_MODEL), F32),
        (row(D_MODEL), (rows, D_MODEL), F32),
    ]
    return pl.pallas_call(
        _proj_prompt_kernel,
        grid=(rows // tm,),
        in_specs=_proj_in_specs(tm, tab.shape[0] // tm),
        out_specs=[s for s, _, _ in specs],
        out_shape=[jax.ShapeDtypeStruct(shape, dtype) for _, shape, dtype in specs],
        scratch_shapes=[pltpu.VMEM((tm, D_MODEL), BF16)],
        compiler_params=_params("parallel"),
        name="proj_prompt",
    )(x2d, gain.reshape(1, D_MODEL), w_bf16, tab)


def _lambda(lam_ref, lam_init):
    lv = lam_ref[...]
    s1 = jnp.sum(lv[0:1] * lv[1:2], axis=-1, keepdims=True)
    s2 = jnp.sum(lv[2:3] * lv[3:4], axis=-1, keepdims=True)
    return jnp.exp(s1) - jnp.exp(s2) + lam_init


def _dot_nt(a, b):
    return lax.dot_general(a, b, (((1,), (1,)), ((), ())), preferred_element_type=F32)


def _sub_layer_norm(o, sub_ref, lam_init):
    return _rms(o) * sub_ref[...] * (1.0 - lam_init)


def _attn_prompt_kernel(lam_ref, sub_ref, qx_ref, k_ref, vt_ref, o_ref, m_ref, acc_ref, st_ref, mx_ref, *, lam_init):
    qi = pl.program_id(1)
    ki = pl.program_id(2)
    n_stat = 2 * DIFF_HEADS
    tk = k_ref.shape[0]

    @pl.when(ki == 0)
    def _():
        m_ref[...] = jnp.full(m_ref.shape, NEG_INF, F32)
        acc_ref[...] = jnp.zeros(acc_ref.shape, F32)

    def scores(i, on_diagonal):
        h = i // 2
        k_h = k_ref[:, h * DIFF_V_DIM:(h + 1) * DIFF_V_DIM]
        st = jnp.dot(k_h, qx_ref[i * DIFF_V_DIM:(i + 1) * DIFF_V_DIM, :], preferred_element_type=F32)
        if on_diagonal:
            key = lax.broadcasted_iota(jnp.int32, st.shape, 0)
            qry = lax.broadcasted_iota(jnp.int32, st.shape, 1)
            st = jnp.where(key <= qry, st, NEG_INF)
        st_ref[i % 2] = st
        mx_ref[i % 2] = jnp.max(st, axis=0, keepdims=True)

    def step(on_diagonal):
        ones = jnp.ones((ACC_ROWS - DIFF_V_DIM, tk), BF16)
        scores(0, on_diagonal)
        for i in range(n_stat):
            if i + 1 < n_stat:
                scores(i + 1, on_diagonal)
            h = i // 2
            vt_h = jnp.concatenate([vt_ref[h * DIFF_V_DIM:(h + 1) * DIFF_V_DIM, :], ones], axis=0)
            m_prev = m_ref[i:i + 1, :]
            m_new = jnp.maximum(m_prev, mx_ref[i % 2])
            p = jnp.exp2(st_ref[i % 2] - m_new).astype(BF16)
            acc_ref[i] = jnp.exp2(m_prev - m_new) * acc_ref[i] + jnp.dot(vt_h, p, preferred_element_type=F32)
            m_ref[i:i + 1, :] = m_new

    def normalized(i):
        return acc_ref[i, 0:DIFF_V_DIM, :] / acc_ref[i, DIFF_V_DIM:DIFF_V_DIM + 1, :]

    @pl.when(ki < qi)
    def _():
        step(False)

    @pl.when(ki == qi)
    def _():
        step(True)
        lam = _lambda(lam_ref, lam_init)
        for h in range(DIFF_HEADS):
            ot = normalized(2 * h) - lam * normalized(2 * h + 1)
            o_ref[:, h * DIFF_V_DIM:(h + 1) * DIFF_V_DIM] = _sub_layer_norm(ot.T, sub_ref, lam_init).astype(o_ref.dtype)


def _attn_prompt(lams, subln, qx, kn, vt, batch, seq, lam_init, blk=512):
    nb = seq // blk
    n_stat = 2 * DIFF_HEADS
    return pl.pallas_call(
        functools.partial(_attn_prompt_kernel, lam_init=lam_init),
        grid=(batch, nb, nb),
        in_specs=[
            _resident((4, DIFF_HEAD_DIM)),
            _resident((1, DIFF_V_DIM)),
            pl.BlockSpec((None, 2 * DIFF_QK_W, blk), lambda b, qi, ki: (b, 0, qi)),
            pl.BlockSpec((blk, DIFF_QK_W), lambda b, qi, ki: (b * nb + jnp.minimum(ki, qi), 0)),
            pl.BlockSpec((None, DIFF_V_W, blk), lambda b, qi, ki: (b, 0, jnp.minimum(ki, qi))),
        ],
        out_specs=pl.BlockSpec((blk, DIFF_V_W), lambda b, qi, ki: (b * nb + qi, 0)),
        out_shape=jax.ShapeDtypeStruct((batch * seq, DIFF_V_W), BF16),
        scratch_shapes=[
            pltpu.VMEM((n_stat, blk), F32),
            pltpu.VMEM((n_stat, ACC_ROWS, blk), F32),
            pltpu.VMEM((2, blk, blk), F32),
            pltpu.VMEM((2, 1, blk), F32),
        ],
        compiler_params=_params("parallel", "parallel", "arbitrary"),
        name="attn_prompt",
    )(lams, subln, qx, kn, vt)


def _attn_sample_kernel(pt_ref, lam_ref, sub_ref, q_ref, kn_ref, vn_ref, *rest, n_pages, page, lam_init):
    del pt_ref
    kt_pages = rest[:n_pages]
    v_pages = rest[n_pages:2 * n_pages]
    o_ref = rest[2 * n_pages]
    t = q_ref.shape[0]
    lam = _lambda(lam_ref, lam_init)
    pad = jnp.zeros((page - t, DIFF_V_DIM), BF16)
    row = lax.broadcasted_iota(jnp.int32, (2 * t, page), 0) % t
    col = lax.broadcasted_iota(jnp.int32, (2 * t, page), 1)
    lane = lax.broadcasted_iota(jnp.int32, (t, DIFF_V_DIM), 1)
    outs = []
    for h in range(DIFF_HEADS):
        hs = slice(h * DIFF_V_DIM, (h + 1) * DIFF_V_DIM)
        qh = q_ref[:, hs].astype(BF16)
        zero = jnp.zeros_like(qh)
        q = jnp.concatenate([jnp.where(lane < DIFF_HEAD_DIM, qh, zero), jnp.where(lane >= DIFF_HEAD_DIM, qh, zero)],
                            axis=0)
        k_new = jnp.concatenate([kn_ref[:, hs].astype(BF16), pad], axis=0)
        v_new = jnp.concatenate([vn_ref[:, hs].astype(BF16), pad], axis=0)
        s_new = jnp.where(col <= row, _dot_nt(q, k_new), NEG_INF)
        s = jnp.concatenate(
            [jnp.dot(q, kp[hs, :].astype(BF16), preferred_element_type=F32) for kp in kt_pages] + [s_new], axis=1)
        p = jnp.exp(s - jnp.max(s, axis=1, keepdims=True))
        l = jnp.sum(p, axis=1, keepdims=True)
        p = p.astype(BF16)
        acc = jnp.dot(p[:, n_pages * page:], v_new, preferred_element_type=F32)
        for j, vp in enumerate(v_pages):
            v_h = vp[pl.ds(h, page, stride=DIFF_HEADS), :].astype(BF16)
            acc += jnp.dot(p[:, j * page:(j + 1) * page], v_h, preferred_element_type=F32)
        o = acc / l
        outs.append(_sub_layer_norm(o[:t] - lam * o[t:], sub_ref, lam_init))
    o_ref[...] = jnp.concatenate(outs, axis=1).astype(o_ref.dtype)


def _attn_sample(page_table, lams, subln, qa, ka, va, kt_pages, v_pages, dec_seq, lam_init):
    dec_batch, n_pages = page_table.shape
    page = kt_pages.shape[2]
    tok_map = lambda b, pt: (b, 0)

    def page_spec(j):
        return pl.BlockSpec((None, DIFF_QK_W, page), lambda b, pt: (pt[b * n_pages + j], 0, 0))

    return pl.pallas_call(
        functools.partial(_attn_sample_kernel, n_pages=n_pages, page=page, lam_init=lam_init),
        grid_spec=pltpu.PrefetchScalarGridSpec(
            num_scalar_prefetch=1,
            grid=(dec_batch,),
            in_specs=[
                pl.BlockSpec((4, DIFF_HEAD_DIM), lambda b, pt: (0, 0)),
                pl.BlockSpec((1, DIFF_V_DIM), lambda b, pt: (0, 0)),
                pl.BlockSpec((dec_seq, DIFF_QK_W), tok_map),
                pl.BlockSpec((dec_seq, DIFF_QK_W), tok_map),
                pl.BlockSpec((dec_seq, DIFF_V_W), tok_map),
            ] + [page_spec(j) for j in range(n_pages)] * 2,
            out_specs=pl.BlockSpec((dec_seq, DIFF_V_W), tok_map),
        ),
        out_shape=jax.ShapeDtypeStruct((dec_batch * dec_seq, DIFF_V_W), F32),
        compiler_params=_params("parallel"),
        name="attn_sample",
    )(page_table.reshape(-1), lams, subln, qa, ka, va, *([kt_pages] * n_pages), *([v_pages] * n_pages))


def _iota_f32(shape, axis):
    return lax.broadcasted_iota(jnp.int32, shape, axis).astype(F32)


def _ret_decays(c, dk, dv, lg):
    rel = _iota_f32((c, c), 0) - _iota_f32((c, c), 1)
    decay = jnp.where(rel >= 0, jnp.exp(lg * jnp.maximum(rel, 0.0)), 0.0)
    q_dec = jnp.exp((_iota_f32((c, dk), 0) + 1.0) * lg)
    k_dec = jnp.exp((c - 1.0 - _iota_f32((c, dk), 0)) * lg)
    s_dec = jnp.exp(jnp.full((1, dv), float(c), F32) * lg)
    return decay, q_dec, k_dec, s_dec


def _ret_chunk(s_prev, q, k, v, decays):
    decay, q_dec, k_dec, s_dec = decays
    qk = _dot_nt(q.astype(BF16), k.astype(BF16)) * decay
    o = jnp.dot(qk.astype(BF16), v, preferred_element_type=F32)
    o = o + jnp.dot((q * q_dec).astype(BF16), s_prev.astype(BF16), preferred_element_type=F32)
    kv = jnp.dot((k * k_dec).T.astype(BF16), v, preferred_element_type=F32)
    return s_dec * s_prev + kv, o


def _gated_ret_out(o, gr):
    return _rms(o) * (gr * _sigmoid(gr))


def _ret_prompt_kernel(lg_ref, q_ref, k_ref, v_ref, gr_ref, r_ref, state_ref, s_ref):
    c = pl.program_id(1)

    @pl.when(c == 0)
    def _():
        s_ref[...] = jnp.zeros(s_ref.shape, F32)

    decays = _ret_decays(RET_CHUNK, RET_KEY_DIM, RET_VALUE_DIM, lg_ref[pl.program_id(0)])
    for b in range(q_ref.shape[0]):
        s_new, o = _ret_chunk(s_ref[b], q_ref[b].astype(F32), k_ref[b].astype(F32), v_ref[b], decays)
        s_ref[b] = s_new
        r_ref[b] = _gated_ret_out(o, gr_ref[b]).astype(r_ref.dtype)

    @pl.when(c == pl.num_programs(1) - 1)
    def _():
        state_ref[...] = s_ref[...]


def _ret_prompt(log_gamma, qr, kr, vr, gr, batch, seq):
    tok = lambda w: pl.BlockSpec((batch, RET_CHUNK, w), lambda h, c: (0, c, h))
    return pl.pallas_call(
        _ret_prompt_kernel,
        grid=(RET_HEADS, seq // RET_CHUNK),
        in_specs=[pl.BlockSpec(memory_space=pltpu.SMEM), tok(RET_KEY_DIM), tok(RET_KEY_DIM), tok(RET_VALUE_DIM),
                  tok(RET_VALUE_DIM)],
        out_specs=[
            tok(RET_VALUE_DIM),
            pl.BlockSpec((batch, None, RET_KEY_DIM, RET_VALUE_DIM), lambda h, c: (0, h, 0, 0)),
        ],
        out_shape=[
            jax.ShapeDtypeStruct((batch, seq, RET_V_W), BF16),
            jax.ShapeDtypeStruct((batch, RET_HEADS, RET_KEY_DIM, RET_VALUE_DIM), F32),
        ],
        scratch_shapes=[pltpu.VMEM((batch, RET_KEY_DIM, RET_VALUE_DIM), F32)],
        compiler_params=_params("parallel", "arbitrary"),
        name="ret_prompt",
    )(log_gamma, qr, kr, vr, gr)


def _ret_sample_kernel(lg_ref, s_ref, q_ref, k_ref, v_ref, gr_ref, r_ref, state_ref):
    nb = s_ref.shape[0]
    t = q_ref.shape[0] // nb
    for h in range(RET_HEADS):
        ks = slice(h * RET_KEY_DIM, (h + 1) * RET_KEY_DIM)
        vs = slice(h * RET_VALUE_DIM, (h + 1) * RET_VALUE_DIM)
        decays = _ret_decays(t, RET_KEY_DIM, RET_VALUE_DIM, lg_ref[h])
        for b in range(nb):
            rows = slice(b * t, (b + 1) * t)
            s_new, o = _ret_chunk(s_ref[b, h], q_ref[rows, ks], k_ref[rows, ks], v_ref[rows, vs].astype(BF16), decays)
            state_ref[b, h] = s_new
            r_ref[rows, vs] = _gated_ret_out(o, gr_ref[rows, vs]).astype(r_ref.dtype)


def _ret_sample(log_gamma, state, qr, kr, vr, gr, dec_seq, nb=2):
    dec_batch = state.shape[0]
    tok_map = lambda b: (b, 0)
    state_spec = pl.BlockSpec((nb, RET_HEADS, RET_KEY_DIM, RET_VALUE_DIM), lambda b: (b, 0, 0, 0))
    return pl.pallas_call(
        _ret_sample_kernel,
        grid=(dec_batch // nb,),
        in_specs=[
            pl.BlockSpec(memory_space=pltpu.SMEM),
            state_spec,
            pl.BlockSpec((nb * dec_seq, RET_QK_W), tok_map),
            pl.BlockSpec((nb * dec_seq, RET_QK_W), tok_map),
            pl.BlockSpec((nb * dec_seq, RET_V_W), tok_map),
            pl.BlockSpec((nb * dec_seq, RET_V_W), tok_map),
        ],
        out_specs=[pl.BlockSpec((nb * dec_seq, RET_V_W), tok_map), state_spec],
        out_shape=[
            jax.ShapeDtypeStruct((dec_batch * dec_seq, RET_V_W), F32),
            jax.ShapeDtypeStruct(state.shape, F32),
        ],
        compiler_params=_params("parallel"),
        name="ret_sample",
    )(log_gamma, state, qr, kr, vr, gr)


def _merge_ffn_kernel(h_ref, a_ref, r_ref, ga_ref, gb_ref, bg_ref, wpa_ref, wpb_ref, wo_ref, gffn_ref, wgu_ref,
                      wdn_ref, gfin_ref, y_ref):
    pa = jnp.dot(a_ref[...].astype(BF16), wpa_ref[...], preferred_element_type=F32)
    pb = jnp.dot(r_ref[...].astype(BF16), wpb_ref[...], preferred_element_type=F32)
    m = (_sigmoid(ga_ref[...] + bg_ref[:, :D_MODEL]) * pa + _sigmoid(gb_ref[...] + bg_ref[:, D_MODEL:]) * pb)
    h1 = h_ref[...] + jnp.dot(m.astype(BF16), wo_ref[...], preferred_element_type=F32)
    xn = (_rms(h1) * gffn_ref[...]).astype(BF16)
    gu = jnp.dot(xn, wgu_ref[...], preferred_element_type=F32)
    g, u = gu[:, :FFN_HIDDEN], gu[:, FFN_HIDDEN:]
    hid = (g * _sigmoid(g) * u).astype(BF16)
    h2 = h1 + jnp.dot(hid, wdn_ref[...], preferred_element_type=F32)
    y_ref[...] = _rms(h2) * gfin_ref[...]


def _merge_ffn(h2d, a, r, ga, gb, b_gate, w_pa, w_pb, w_o, g_ffn, w_gu, w_dn, g_fin, tm):
    rows = h2d.shape[0]
    row = lambda w: pl.BlockSpec((tm, w), lambda i: (i, 0))
    return pl.pallas_call(
        _merge_ffn_kernel,
        grid=(rows // tm,),
        in_specs=[
            row(D_MODEL), row(DIFF_V_W), row(RET_V_W), row(D_MODEL), row(D_MODEL),
            _resident((1, 2 * D_MODEL)),
            _resident((DIFF_V_W, D_MODEL)), _resident((RET_V_W, D_MODEL)), _resident((D_MODEL, D_MODEL)),
            _resident((1, D_MODEL)),
            _resident((D_MODEL, 2 * FFN_HIDDEN)), _resident((FFN_HIDDEN, D_MODEL)),
            _resident((1, D_MODEL)),
        ],
        out_specs=row(D_MODEL),
        out_shape=jax.ShapeDtypeStruct((rows, D_MODEL), F32),
        compiler_params=_params("parallel"),
        name="merge_ffn",
    )(h2d, a, r, ga, gb, b_gate.reshape(1, -1), w_pa, w_pb, w_o, g_ffn.reshape(1, -1), w_gu, w_dn,
      g_fin.reshape(1, -1))


def kernel(x_prompt, x_sample, cache_k, cache_v, state_ret, page_table, norm_mix_g, w_in, b_gate, lambda_q1, lambda_k1,
           lambda_q2, lambda_k2, subln_g, w_pa, w_pb, w_o, norm_ffn_g, w_gu, w_down, norm_final_g):
    batch, seq, _ = x_prompt.shape
    dec_batch, dec_seq, _ = x_sample.shape
    depth, n_phys, page = cache_k.shape[:3]
    assert depth == 1, "the final norm is fused into the (single) layer's FFN kernel"
    past = page_table.shape[1] * page
    tm = 256

    log_gamma = jnp.log(1.0 - jnp.exp2(-5.0 - jnp.arange(RET_HEADS, dtype=F32)))
    tab_p = _rotation_tables(jnp.arange(seq, dtype=F32))
    tab_s = jnp.tile(_rotation_tables(jnp.arange(dec_seq, dtype=F32) + past), (tm // dec_seq, 1))

    l = 0
    lam_init = _lam_init(l)
    lams = jnp.stack([lambda_q1[l], lambda_k1[l], lambda_q2[l], lambda_k2[l]]).astype(F32)
    subln = subln_g[l].reshape(1, DIFF_V_DIM).astype(F32)
    w_in_b = w_in[l].astype(BF16)
    weights = (b_gate[l], w_pa[l].astype(BF16), w_pb[l].astype(BF16), w_o[l].astype(BF16), norm_ffn_g[l],
               w_gu[l].astype(BF16), w_down[l].astype(BF16), norm_final_g)
    kt_pages = jnp.transpose(cache_k[l], (0, 2, 3, 4, 1)).reshape(n_phys, DIFF_QK_W, page)
    v_pages = cache_v[l].reshape(n_phys, page * DIFF_HEADS, DIFF_V_DIM)

    hp = x_prompt.reshape(batch * seq, D_MODEL)
    qx, kn, kt_p, vt, v_p, qr, kr, vr, gr, g_a, g_b = _project_prompt(hp, norm_mix_g[l], w_in_b, tab_p, batch, seq, tm)
    a = _attn_prompt(lams, subln, qx, kn, vt, batch, seq, lam_init)
    as_seq = lambda x: x.reshape(batch, seq, x.shape[-1])
    r, s_p = _ret_prompt(log_gamma, as_seq(qr), as_seq(kr), as_seq(vr), as_seq(gr), batch, seq)
    y_p = _merge_ffn(hp, a, r.reshape(batch * seq, RET_V_W), g_a, g_b, *weights, tm)

    hs = x_sample.reshape(dec_batch * dec_seq, D_MODEL)
    qa, ka_s, va_s, qr, kr, vr, gr, g_a, g_b = _project_sample(hs, norm_mix_g[l], w_in_b, tab_s, tm)
    a = _attn_sample(page_table, lams, subln, qa, ka_s, va_s, kt_pages, v_pages, dec_seq, lam_init)
    r, s_s = _ret_sample(log_gamma, state_ret[l], qr, kr, vr, gr, dec_seq)
    y_s = _merge_ffn(hs, a, r, g_a, g_b, *weights, tm)

    k_p = jnp.transpose(kt_p.reshape(1, batch, DIFF_HEADS, 2, DIFF_HEAD_DIM, seq), (0, 1, 5, 2, 3, 4))
    return (
        y_p.reshape(batch, seq, D_MODEL),
        y_s.reshape(dec_batch, dec_seq, D_MODEL),
        k_p,
        v_p.reshape(1, batch, seq, DIFF_HEADS, DIFF_V_DIM),
        s_p[None],
        ka_s.reshape(1, dec_batch, dec_seq, DIFF_HEADS, 2, DIFF_HEAD_DIM),
        va_s.reshape(1, dec_batch, dec_seq, DIFF_HEADS, DIFF_V_DIM),
        s_s[None],
    )
```

```python
import functools
import math

import jax
import jax.numpy as jnp
import numpy as np
from jax import lax
from jax.experimental import pallas as pl
from jax.experimental.pallas import tpu as pltpu

F32 = jnp.float32
BF16 = jnp.bfloat16

D_MODEL = 1024
DIFF_HEADS = 8
DIFF_HEAD_DIM = 64
DIFF_V_DIM = 2 * DIFF_HEAD_DIM
ROT_DIM = DIFF_HEAD_DIM // 4
ROPE_THETA = 500000.0
RET_HEADS = 4
RET_KEY_DIM = 256
RET_VALUE_DIM = 512
RET_THETA = 10000.0
RET_CHUNK = 128
FFN_HIDDEN = 2816
NORM_EPS = 1e-5
NEG_INF = -1e30

DIFF_QK_W = DIFF_HEADS * 2 * DIFF_HEAD_DIM
DIFF_V_W = DIFF_HEADS * DIFF_V_DIM
RET_QK_W = RET_HEADS * RET_KEY_DIM
RET_V_W = RET_HEADS * RET_VALUE_DIM
IN_W = 3 * DIFF_QK_W + 2 * RET_QK_W + 2 * RET_V_W + 2 * D_MODEL

LANES = 128
ROPE_TAB_W = 3 * LANES
RET_TAB_W = 3 * RET_KEY_DIM
TAB_W = ROPE_TAB_W + RET_TAB_W
VMEM_LIMIT = 56 * 1024 * 1024
BF16_SUBLANES = 16
ACC_ROWS = DIFF_V_DIM + BF16_SUBLANES
LOG2E = math.log2(math.e)


def _lam_init(layer):
    return 0.8 - 0.6 * math.exp(-0.3 * layer)


def _sigmoid(x):
    return 1.0 / (1.0 + jnp.exp(-x))


def _rms(x):
    return x * lax.rsqrt(jnp.mean(x * x, axis=-1, keepdims=True) + NORM_EPS)


def _resident(shape):
    return pl.BlockSpec(shape, lambda *_: (0,) * len(shape), pipeline_mode=pl.Buffered(1))


def _params(*semantics):
    return pltpu.CompilerParams(dimension_semantics=semantics, vmem_limit_bytes=VMEM_LIMIT)


def _rotation_tables(pos):
    half = ROT_DIM // 2
    freqs = jnp.exp(-math.log(ROPE_THETA) * jnp.arange(half, dtype=F32) * (2.0 / ROT_DIM))
    ang = pos[:, None] * freqs[None, :]
    cos, sin = jnp.cos(ang), jnp.sin(ang)
    lane = np.arange(LANES) % DIFF_HEAD_DIM
    idx = lane % half
    rope_c = jnp.where(lane < ROT_DIM, cos[:, idx], 1.0)
    rope_s1 = jnp.where(lane < half, -sin[:, idx], 0.0)
    rope_s2 = jnp.where((lane >= half) & (lane < ROT_DIM), sin[:, idx], 0.0)

    rhalf = RET_KEY_DIM // 2
    angle = jnp.exp(-math.log(RET_THETA) * jnp.linspace(0.0, 1.0, rhalf, dtype=F32))
    rang = pos[:, None] * angle[None, :]
    rcos, rsin = jnp.cos(rang), jnp.sin(rang)
    lane2 = np.arange(RET_KEY_DIM)
    pair = lane2 // 2
    even = lane2 % 2 == 0
    ret_c = rcos[:, pair]
    ret_s1 = jnp.where(even, -rsin[:, pair], 0.0)
    ret_s2 = jnp.where(~even, rsin[:, pair], 0.0)
    return jnp.concatenate([rope_c, rope_s1, rope_s2, ret_c, ret_s1, ret_s2], axis=1).astype(F32)


def _partial_rope(acc, tab):
    c, s1, s2 = tab[:, 0:LANES], tab[:, LANES:2 * LANES], tab[:, 2 * LANES:3 * LANES]
    half = ROT_DIM // 2
    outs = []
    for g in range(acc.shape[1] // LANES):
        x = acc[:, g * LANES:(g + 1) * LANES]
        outs.append(x * c + pltpu.roll(x, LANES - half, 1) * s1 + pltpu.roll(x, half, 1) * s2)
    return jnp.concatenate(outs, axis=1)


def _ret_rotate(acc, tab):
    per_head = RET_KEY_DIM // LANES
    outs = []
    for g in range(acc.shape[1] // LANES):
        o = (g % per_head) * LANES
        c = tab[:, o:o + LANES]
        s1 = tab[:, RET_KEY_DIM + o:RET_KEY_DIM + o + LANES]
        s2 = tab[:, 2 * RET_KEY_DIM + o:2 * RET_KEY_DIM + o + LANES]
        x = acc[:, g * LANES:(g + 1) * LANES]
        outs.append(x * c + pltpu.roll(x, LANES - 1, 1) * s1 + pltpu.roll(x, 1, 1) * s2)
    return jnp.concatenate(outs, axis=1)


class _Projector:
    def __init__(self, x_ref, g_ref, w_ref, tab_ref, xn_ref):
        xn_ref[...] = (_rms(x_ref[...]) * g_ref[...]).astype(BF16)
        self._xn_ref, self._w_ref = xn_ref, w_ref
        self._rope_tab = tab_ref[:, 0:ROPE_TAB_W]
        self._ret_tab = tab_ref[:, ROPE_TAB_W:TAB_W]
        self._col = 0

    def _mm(self, n):
        c0, self._col = self._col, self._col + n
        return jnp.dot(self._xn_ref[...], self._w_ref[:, c0:c0 + n], preferred_element_type=F32)

    def qa(self):
        return _partial_rope(self._mm(DIFF_QK_W), self._rope_tab) * (DIFF_HEAD_DIM ** -0.5)

    def ka(self):
        return _partial_rope(self._mm(DIFF_QK_W), self._rope_tab)

    def va(self):
        return self._mm(DIFF_V_W)

    def qr(self):
        return _ret_rotate(self._mm(RET_QK_W), self._ret_tab)

    def kr(self):
        return _ret_rotate(self._mm(RET_QK_W), self._ret_tab) * (RET_KEY_DIM ** -0.5)

    def wide(self):
        return self._mm(RET_V_W)

    def gate(self):
        return self._mm(D_MODEL)


def _proj_sample_kernel(x_ref, g_ref, w_ref, tab_ref, qa_ref, ka_ref, va_ref, qr_ref, kr_ref, vr_ref, gr_ref, ga_ref,
                        gb_ref, xn_ref):
    p = _Projector(x_ref, g_ref, w_ref, tab_ref, xn_ref)
    qa_ref[...] = p.qa()
    ka_ref[...] = p.ka()
    va_ref[...] = p.va()
    qr_ref[...] = p.qr()
    kr_ref[...] = p.kr()
    vr_ref[...] = p.wide()
    gr_ref[...] = p.wide()
    ga_ref[...] = p.gate()
    gb_ref[...] = p.gate()


def _proj_prompt_kernel(x_ref, g_ref, w_ref, tab_ref, qx_ref, kn_ref, kt_ref, vt_ref, vo_ref, qr_ref, kr_ref, vr_ref,
                        gr_ref, ga_ref, gb_ref, xn_ref):
    p = _Projector(x_ref, g_ref, w_ref, tab_ref, xn_ref)
    tm = x_ref.shape[0]
    qt = (p.qa() * LOG2E).T.astype(BF16)
    zeros = jnp.zeros((DIFF_HEAD_DIM, tm), BF16)
    for hc in range(2 * DIFF_HEADS):
        comp = hc % 2
        src = qt[hc * DIFF_HEAD_DIM:(hc + 1) * DIFF_HEAD_DIM]
        lo, hi = (src, zeros) if comp == 0 else (zeros, src)
        qx_ref[hc * DIFF_V_DIM:hc * DIFF_V_DIM + DIFF_HEAD_DIM, :] = lo
        qx_ref[hc * DIFF_V_DIM + DIFF_HEAD_DIM:(hc + 1) * DIFF_V_DIM, :] = hi
    k = p.ka()
    kn_ref[...] = k.astype(BF16)
    kt_ref[...] = k.T
    v = p.va()
    vt_ref[...] = v.T.astype(BF16)
    for h in range(DIFF_HEADS):
        vo_ref[pl.ds(h, tm, stride=DIFF_HEADS), :] = v[:, h * DIFF_V_DIM:(h + 1) * DIFF_V_DIM]
    qr_ref[...] = p.qr().astype(BF16)
    kr_ref[...] = p.kr().astype(BF16)
    vr_ref[...] = p.wide().astype(BF16)
    gr_ref[...] = p.wide()
    ga_ref[...] = p.gate()
    gb_ref[...] = p.gate()


def _proj_in_specs(tm, tab_blocks):
    return [
        pl.BlockSpec((tm, D_MODEL), lambda i: (i, 0)),
        _resident((1, D_MODEL)),
        _resident((D_MODEL, IN_W)),
        pl.BlockSpec((tm, TAB_W), lambda i: (i % tab_blocks, 0)),
    ]


def _project_sample(x2d, gain, w_bf16, tab, tm):
    rows = x2d.shape[0]
    widths = (DIFF_QK_W, DIFF_QK_W, DIFF_V_W, RET_QK_W, RET_QK_W, RET_V_W, RET_V_W, D_MODEL, D_MODEL)
    return pl.pallas_call(
        _proj_sample_kernel,
        grid=(rows // tm,),
        in_specs=_proj_in_specs(tm, tab.shape[0] // tm),
        out_specs=[pl.BlockSpec((tm, w), lambda i: (i, 0)) for w in widths],
        out_shape=[jax.ShapeDtypeStruct((rows, w), F32) for w in widths],
        scratch_shapes=[pltpu.VMEM((tm, D_MODEL), BF16)],
        compiler_params=_params("parallel"),
        name="proj_sample",
    )(x2d, gain.reshape(1, D_MODEL), w_bf16, tab)


def _project_prompt(x2d, gain, w_bf16, tab, batch, seq, tm):
    rows = batch * seq
    per_b = seq // tm
    row = lambda w: pl.BlockSpec((tm, w), lambda i: (i, 0))
    transposed = lambda h: pl.BlockSpec((None, h, tm), lambda i: (i // per_b, 0, i % per_b))
    specs = [
        (transposed(2 * DIFF_QK_W), (batch, 2 * DIFF_QK_W, seq), BF16),
        (row(DIFF_QK_W), (rows, DIFF_QK_W), BF16),
        (transposed(DIFF_QK_W), (batch, DIFF_QK_W, seq), F32),
        (transposed(DIFF_V_W), (batch, DIFF_V_W, seq), BF16),
        (pl.BlockSpec((tm * DIFF_HEADS, DIFF_V_DIM), lambda i: (i, 0)), (rows * DIFF_HEADS, DIFF_V_DIM), F32),
        (row(RET_QK_W), (rows, RET_QK_W), BF16),
        (row(RET_QK_W), (rows, RET_QK_W), BF16),
        (row(RET_V_W), (rows, RET_V_W), BF16),
        (row(RET_V_W), (rows, RET_V_W), F32),
        (row(D_MODEL), (rows, D_MODEL), F32),
        (row(D_MODEL), (rows, D_MODEL), F32),
    ]
    return pl.pallas_call(
        _proj_prompt_kernel,
        grid=(rows // tm,),
        in_specs=_proj_in_specs(tm, tab.shape[0] // tm),
        out_specs=[s for s, _, _ in specs],
        out_shape=[jax.ShapeDtypeStruct(shape, dtype) for _, shape, dtype in specs],
        scratch_shapes=[pltpu.VMEM((tm, D_MODEL), BF16)],
        compiler_params=_params("parallel"),
        name="proj_prompt",
    )(x2d, gain.reshape(1, D_MODEL), w_bf16, tab)


def _lambda(lam_ref, lam_init):
    lv = lam_ref[...]
    s1 = jnp.sum(lv[0:1] * lv[1:2], axis=-1, keepdims=True)
    s2 = jnp.sum(lv[2:3] * lv[3:4], axis=-1, keepdims=True)
    return jnp.exp(s1) - jnp.exp(s2) + lam_init


def _dot_nt(a, b):
    return lax.dot_general(a, b, (((1,), (1,)), ((), ())), preferred_element_type=F32)


def _sub_layer_norm(o, sub_ref, lam_init):
    return _rms(o) * sub_ref[...] * (1.0 - lam_init)


def _attn_kernel(pt_ref, qi_ref, ki_ref, lam_ref, sub_ref, qx_ref, k_ref, vt_ref, qs_ref, kn_ref, vn_ref, *rest,
                 n_pages, lam_init):
    del pt_ref
    kt_pages = rest[:n_pages]
    v_pages = rest[n_pages:2 * n_pages]
    o_ref, os_ref, m_ref, acc_ref, st_ref, mx_ref = rest[2 * n_pages:]
    qi = qi_ref[pl.program_id(1)]
    ki = ki_ref[pl.program_id(1)]
    n_stat = 2 * DIFF_HEADS
    tk = k_ref.shape[0]
    lam = _lambda(lam_ref, lam_init)

    @pl.when(ki == 0)
    def _():
        m_ref[...] = jnp.full(m_ref.shape, NEG_INF, F32)
        acc_ref[...] = jnp.zeros(acc_ref.shape, F32)

    def scores(i, on_diagonal):
        h = i // 2
        k_h = k_ref[:, h * DIFF_V_DIM:(h + 1) * DIFF_V_DIM]
        st = jnp.dot(k_h, qx_ref[i * DIFF_V_DIM:(i + 1) * DIFF_V_DIM, :], preferred_element_type=F32)
        if on_diagonal:
            key = lax.broadcasted_iota(jnp.int32, st.shape, 0)
            qry = lax.broadcasted_iota(jnp.int32, st.shape, 1)
            st = jnp.where(key <= qry, st, NEG_INF)
        st_ref[i % 2] = st
        mx_ref[i % 2] = jnp.max(st, axis=0, keepdims=True)

    def step(on_diagonal):
        ones = jnp.ones((ACC_ROWS - DIFF_V_DIM, tk), BF16)
        scores(0, on_diagonal)
        for i in range(n_stat):
            if i + 1 < n_stat:
                scores(i + 1, on_diagonal)
            h = i // 2
            vt_h = jnp.concatenate([vt_ref[h * DIFF_V_DIM:(h + 1) * DIFF_V_DIM, :], ones], axis=0)
            m_prev = m_ref[i:i + 1, :]
            m_new = jnp.maximum(m_prev, mx_ref[i % 2])
            p = jnp.exp2(st_ref[i % 2] - m_new).astype(BF16)
            acc_ref[i] = jnp.exp2(m_prev - m_new) * acc_ref[i] + jnp.dot(vt_h, p, preferred_element_type=F32)
            m_ref[i:i + 1, :] = m_new
            if i % 2 == 1:
                _sample_attn_head(h, lam, sub_ref, qs_ref, kn_ref, vn_ref, kt_pages, v_pages, os_ref, lam_init)

    def normalized(i):
        return acc_ref[i, 0:DIFF_V_DIM, :] / acc_ref[i, DIFF_V_DIM:DIFF_V_DIM + 1, :]

    @pl.when(ki < qi)
    def _():
        step(False)

    @pl.when(ki == qi)
    def _():
        step(True)
        for h in range(DIFF_HEADS):
            ot = normalized(2 * h) - lam * normalized(2 * h + 1)
            o_ref[:, h * DIFF_V_DIM:(h + 1) * DIFF_V_DIM] = _sub_layer_norm(ot.T, sub_ref, lam_init).astype(o_ref.dtype)


def _attention(page_table, lams, subln, qx, kn, vt, qa_s, ka_s, va_s, kt_pages, v_pages, batch, seq, dec_seq, lam_init,
               blk=512):
    nb = seq // blk
    n_stat = 2 * DIFF_HEADS
    dec_batch, n_pages = page_table.shape
    page = kt_pages.shape[2]
    tiles = [(qi, ki) for qi in range(nb) for ki in range(qi + 1)]
    assert batch * len(tiles) >= dec_batch
    qi_of = jnp.asarray([qi for qi, _ in tiles], jnp.int32)
    ki_of = jnp.asarray([ki for _, ki in tiles], jnp.int32)

    def seq_of(b, t):
        return jnp.minimum(b * len(tiles) + t, dec_batch - 1)

    tok_map = lambda b, t, pt, qo, ko: (seq_of(b, t), 0)

    def page_spec(j):
        return pl.BlockSpec((None, DIFF_QK_W, page), lambda b, t, pt, qo, ko: (pt[seq_of(b, t) * n_pages + j], 0, 0))

    const = lambda shape: pl.BlockSpec(shape, lambda b, t, pt, qo, ko: (0,) * len(shape))
    return pl.pallas_call(
        functools.partial(_attn_kernel, n_pages=n_pages, lam_init=lam_init),
        grid_spec=pltpu.PrefetchScalarGridSpec(
            num_scalar_prefetch=3,
            grid=(batch, len(tiles)),
            in_specs=[
                const((4, DIFF_HEAD_DIM)),
                const((1, DIFF_V_DIM)),
                pl.BlockSpec((None, 2 * DIFF_QK_W, blk), lambda b, t, pt, qo, ko: (b, 0, qo[t])),
                pl.BlockSpec((blk, DIFF_QK_W), lambda b, t, pt, qo, ko: (b * nb + ko[t], 0)),
                pl.BlockSpec((None, DIFF_V_W, blk), lambda b, t, pt, qo, ko: (b, 0, ko[t])),
                pl.BlockSpec((dec_seq, DIFF_QK_W), tok_map),
                pl.BlockSpec((dec_seq, DIFF_QK_W), tok_map),
                pl.BlockSpec((dec_seq, DIFF_V_W), tok_map),
            ] + [page_spec(j) for j in range(n_pages)] * 2,
            out_specs=[
                pl.BlockSpec((blk, DIFF_V_W), lambda b, t, pt, qo, ko: (b * nb + qo[t], 0)),
                pl.BlockSpec((dec_seq, DIFF_V_W), tok_map),
            ],
            scratch_shapes=[
                pltpu.VMEM((n_stat, blk), F32),
                pltpu.VMEM((n_stat, ACC_ROWS, blk), F32),
                pltpu.VMEM((2, blk, blk), F32),
                pltpu.VMEM((2, 1, blk), F32),
            ],
        ),
        out_shape=[
            jax.ShapeDtypeStruct((batch * seq, DIFF_V_W), BF16),
            jax.ShapeDtypeStruct((dec_batch * dec_seq, DIFF_V_W), F32),
        ],
        compiler_params=_params("arbitrary", "arbitrary"),
        name="attention",
    )(page_table.reshape(-1), qi_of, ki_of, lams, subln, qx, kn, vt, qa_s, ka_s, va_s,
      *([kt_pages] * n_pages), *([v_pages] * n_pages))


def _sample_attn_head(h, lam, sub_ref, q_ref, kn_ref, vn_ref, kt_pages, v_pages, o_ref, lam_init):
    t = q_ref.shape[0]
    page = kt_pages[0].shape[1]
    n_pages = len(kt_pages)
    hs = slice(h * DIFF_V_DIM, (h + 1) * DIFF_V_DIM)
    pad = jnp.zeros((page - t, DIFF_V_DIM), BF16)
    row = lax.broadcasted_iota(jnp.int32, (2 * t, page), 0) % t
    col = lax.broadcasted_iota(jnp.int32, (2 * t, page), 1)
    lane = lax.broadcasted_iota(jnp.int32, (t, DIFF_V_DIM), 1)
    qh = q_ref[:, hs].astype(BF16)
    zero = jnp.zeros_like(qh)
    q = jnp.concatenate([jnp.where(lane < DIFF_HEAD_DIM, qh, zero), jnp.where(lane >= DIFF_HEAD_DIM, qh, zero)], axis=0)
    k_new = jnp.concatenate([kn_ref[:, hs].astype(BF16), pad], axis=0)
    v_new = jnp.concatenate([vn_ref[:, hs].astype(BF16), pad], axis=0)
    s_new = jnp.where(col <= row, _dot_nt(q, k_new), NEG_INF)
    s = jnp.concatenate(
        [jnp.dot(q, kp[hs, :].astype(BF16), preferred_element_type=F32) for kp in kt_pages] + [s_new], axis=1)
    p = jnp.exp(s - jnp.max(s, axis=1, keepdims=True))
    l = jnp.sum(p, axis=1, keepdims=True)
    p = p.astype(BF16)
    acc = jnp.dot(p[:, n_pages * page:], v_new, preferred_element_type=F32)
    for j, vp in enumerate(v_pages):
        v_h = vp[pl.ds(h, page, stride=DIFF_HEADS), :].astype(BF16)
        acc += jnp.dot(p[:, j * page:(j + 1) * page], v_h, preferred_element_type=F32)
    o = acc / l
    o_ref[:, hs] = _sub_layer_norm(o[:t] - lam * o[t:], sub_ref, lam_init).astype(o_ref.dtype)


def _iota_f32(shape, axis):
    return lax.broadcasted_iota(jnp.int32, shape, axis).astype(F32)


def _ret_decays(c, dk, dv, lg):
    rel = _iota_f32((c, c), 0) - _iota_f32((c, c), 1)
    decay = jnp.where(rel >= 0, jnp.exp(lg * jnp.maximum(rel, 0.0)), 0.0)
    q_dec = jnp.exp((_iota_f32((c, dk), 0) + 1.0) * lg)
    k_dec = jnp.exp((c - 1.0 - _iota_f32((c, dk), 0)) * lg)
    s_dec = jnp.exp(jnp.full((1, dv), float(c), F32) * lg)
    return decay, q_dec, k_dec, s_dec


def _ret_chunk(s_prev, q, k, v, decays):
    decay, q_dec, k_dec, s_dec = decays
    qk = _dot_nt(q.astype(BF16), k.astype(BF16)) * decay
    o = jnp.dot(qk.astype(BF16), v, preferred_element_type=F32)
    o = o + jnp.dot((q * q_dec).astype(BF16), s_prev.astype(BF16), preferred_element_type=F32)
    kv = jnp.dot((k * k_dec).T.astype(BF16), v, preferred_element_type=F32)
    return s_dec * s_prev + kv, o


def _gated_ret_out(o, gr):
    return _rms(o) * (gr * _sigmoid(gr))


def _ret_prompt_kernel(lg_ref, q_ref, k_ref, v_ref, gr_ref, r_ref, state_ref, s_ref):
    c = pl.program_id(1)

    @pl.when(c == 0)
    def _():
        s_ref[...] = jnp.zeros(s_ref.shape, F32)

    decay, q_dec, k_dec, s_dec = _ret_decays(RET_CHUNK, RET_KEY_DIM, RET_VALUE_DIM, lg_ref[pl.program_id(0)])
    batch = range(q_ref.shape[0])
    qk = [(_dot_nt(q_ref[b], k_ref[b]) * decay).astype(BF16) for b in batch]
    o = [jnp.dot(qk[b], v_ref[b], preferred_element_type=F32) for b in batch]
    for b in batch:
        qd = (q_ref[b].astype(F32) * q_dec).astype(BF16)
        o[b] = o[b] + jnp.dot(qd, s_ref[b].astype(BF16), preferred_element_type=F32)
    for b in batch:
        kd = (k_ref[b].astype(F32) * k_dec).T.astype(BF16)
        s_ref[b] = s_dec * s_ref[b] + jnp.dot(kd, v_ref[b], preferred_element_type=F32)
    for b in batch:
        r_ref[b] = _gated_ret_out(o[b], gr_ref[b]).astype(r_ref.dtype)

    @pl.when(c == pl.num_programs(1) - 1)
    def _():
        state_ref[...] = s_ref[...]


def _ret_prompt(log_gamma, qr, kr, vr, gr, batch, seq):
    tok = lambda w: pl.BlockSpec((batch, RET_CHUNK, w), lambda h, c: (0, c, h))
    return pl.pallas_call(
        _ret_prompt_kernel,
        grid=(RET_HEADS, seq // RET_CHUNK),
        in_specs=[pl.BlockSpec(memory_space=pltpu.SMEM), tok(RET_KEY_DIM), tok(RET_KEY_DIM), tok(RET_VALUE_DIM),
                  tok(RET_VALUE_DIM)],
        out_specs=[
            tok(RET_VALUE_DIM),
            pl.BlockSpec((batch, None, RET_KEY_DIM, RET_VALUE_DIM), lambda h, c: (0, h, 0, 0)),
        ],
        out_shape=[
            jax.ShapeDtypeStruct((batch, seq, RET_V_W), BF16),
            jax.ShapeDtypeStruct((batch, RET_HEADS, RET_KEY_DIM, RET_VALUE_DIM), F32),
        ],
        scratch_shapes=[pltpu.VMEM((batch, RET_KEY_DIM, RET_VALUE_DIM), F32)],
        compiler_params=_params("parallel", "arbitrary"),
        name="ret_prompt",
    )(log_gamma, qr, kr, vr, gr)


def _ret_sample_kernel(lg_ref, s_ref, q_ref, k_ref, v_ref, gr_ref, r_ref, state_ref):
    nb = s_ref.shape[0]
    t = q_ref.shape[0] // nb
    for h in range(RET_HEADS):
        ks = slice(h * RET_KEY_DIM, (h + 1) * RET_KEY_DIM)
        vs = slice(h * RET_VALUE_DIM, (h + 1) * RET_VALUE_DIM)
        decays = _ret_decays(t, RET_KEY_DIM, RET_VALUE_DIM, lg_ref[h])
        for b in range(nb):
            rows = slice(b * t, (b + 1) * t)
            s_new, o = _ret_chunk(s_ref[b, h], q_ref[rows, ks], k_ref[rows, ks], v_ref[rows, vs].astype(BF16), decays)
            state_ref[b, h] = s_new
            r_ref[rows, vs] = _gated_ret_out(o, gr_ref[rows, vs]).astype(r_ref.dtype)


def _ret_sample(log_gamma, state, qr, kr, vr, gr, dec_seq, nb=2):
    dec_batch = state.shape[0]
    tok_map = lambda b: (b, 0)
    state_spec = pl.BlockSpec((nb, RET_HEADS, RET_KEY_DIM, RET_VALUE_DIM), lambda b: (b, 0, 0, 0))
    return pl.pallas_call(
        _ret_sample_kernel,
        grid=(dec_batch // nb,),
        in_specs=[
            pl.BlockSpec(memory_space=pltpu.SMEM),
            state_spec,
            pl.BlockSpec((nb * dec_seq, RET_QK_W), tok_map),
            pl.BlockSpec((nb * dec_seq, RET_QK_W), tok_map),
            pl.BlockSpec((nb * dec_seq, RET_V_W), tok_map),
            pl.BlockSpec((nb * dec_seq, RET_V_W), tok_map),
        ],
        out_specs=[pl.BlockSpec((nb * dec_seq, RET_V_W), tok_map), state_spec],
        out_shape=[
            jax.ShapeDtypeStruct((dec_batch * dec_seq, RET_V_W), F32),
            jax.ShapeDtypeStruct(state.shape, F32),
        ],
        compiler_params=_params("parallel"),
        name="ret_sample",
    )(log_gamma, state, qr, kr, vr, gr)


def _merge_ffn_kernel(h_ref, a_ref, r_ref, ga_ref, gb_ref, bg_ref, wpa_ref, wpb_ref, wo_ref, gffn_ref, wgu_ref,
                      wdn_ref, gfin_ref, y_ref):
    pa = jnp.dot(a_ref[...].astype(BF16), wpa_ref[...], preferred_element_type=F32)
    pb = jnp.dot(r_ref[...].astype(BF16), wpb_ref[...], preferred_element_type=F32)
    m = (_sigmoid(ga_ref[...] + bg_ref[:, :D_MODEL]) * pa + _sigmoid(gb_ref[...] + bg_ref[:, D_MODEL:]) * pb)
    h1 = h_ref[...] + jnp.dot(m.astype(BF16), wo_ref[...], preferred_element_type=F32)
    xn = (_rms(h1) * gffn_ref[...]).astype(BF16)
    gu = jnp.dot(xn, wgu_ref[...], preferred_element_type=F32)
    g, u = gu[:, :FFN_HIDDEN], gu[:, FFN_HIDDEN:]
    hid = (g * _sigmoid(g) * u).astype(BF16)
    h2 = h1 + jnp.dot(hid, wdn_ref[...], preferred_element_type=F32)
    y_ref[...] = _rms(h2) * gfin_ref[...]


def _merge_ffn(h2d, a, r, ga, gb, b_gate, w_pa, w_pb, w_o, g_ffn, w_gu, w_dn, g_fin, tm):
    rows = h2d.shape[0]
    row = lambda w: pl.BlockSpec((tm, w), lambda i: (i, 0))
    return pl.pallas_call(
        _merge_ffn_kernel,
        grid=(rows // tm,),
        in_specs=[
            row(D_MODEL), row(DIFF_V_W), row(RET_V_W), row(D_MODEL), row(D_MODEL),
            _resident((1, 2 * D_MODEL)),
            _resident((DIFF_V_W, D_MODEL)), _resident((RET_V_W, D_MODEL)), _resident((D_MODEL, D_MODEL)),
            _resident((1, D_MODEL)),
            _resident((D_MODEL, 2 * FFN_HIDDEN)), _resident((FFN_HIDDEN, D_MODEL)),
            _resident((1, D_MODEL)),
        ],
        out_specs=row(D_MODEL),
        out_shape=jax.ShapeDtypeStruct((rows, D_MODEL), F32),
        compiler_params=_params("parallel"),
        name="merge_ffn",
    )(h2d, a, r, ga, gb, b_gate.reshape(1, -1), w_pa, w_pb, w_o, g_ffn.reshape(1, -1), w_gu, w_dn,
      g_fin.reshape(1, -1))


def kernel(x_prompt, x_sample, cache_k, cache_v, state_ret, page_table, norm_mix_g, w_in, b_gate, lambda_q1, lambda_k1,
           lambda_q2, lambda_k2, subln_g, w_pa, w_pb, w_o, norm_ffn_g, w_gu, w_down, norm_final_g):
    batch, seq, _ = x_prompt.shape
    dec_batch, dec_seq, _ = x_sample.shape
    depth, n_phys, page = cache_k.shape[:3]
    assert depth == 1, "the final norm is fused into the (single) layer's FFN kernel"
    past = page_table.shape[1] * page
    tm = 256

    log_gamma = jnp.log(1.0 - jnp.exp2(-5.0 - jnp.arange(RET_HEADS, dtype=F32)))
    tab_p = _rotation_tables(jnp.arange(seq, dtype=F32))
    tab_s = jnp.tile(_rotation_tables(jnp.arange(dec_seq, dtype=F32) + past), (tm // dec_seq, 1))

    l = 0
    lam_init = _lam_init(l)
    lams = jnp.stack([lambda_q1[l], lambda_k1[l], lambda_q2[l], lambda_k2[l]]).astype(F32)
    subln = subln_g[l].reshape(1, DIFF_V_DIM).astype(F32)
    w_in_b = w_in[l].astype(BF16)
    weights = (b_gate[l], w_pa[l].astype(BF16), w_pb[l].astype(BF16), w_o[l].astype(BF16), norm_ffn_g[l],
               w_gu[l].astype(BF16), w_down[l].astype(BF16), norm_final_g)
    kt_pages = jnp.transpose(cache_k[l], (0, 2, 3, 4, 1)).reshape(n_phys, DIFF_QK_W, page)
    v_pages = cache_v[l].reshape(n_phys, page * DIFF_HEADS, DIFF_V_DIM)

    hp = x_prompt.reshape(batch * seq, D_MODEL)
    hs = x_sample.reshape(dec_batch * dec_seq, D_MODEL)
    qx, kn, kt_p, vt, v_p, qr_p, kr_p, vr_p, gr_p, ga_p, gb_p = _project_prompt(
        hp, norm_mix_g[l], w_in_b, tab_p, batch, seq, tm)
    qa_s, ka_s, va_s, qr_s, kr_s, vr_s, gr_s, ga_s, gb_s = _project_sample(hs, norm_mix_g[l], w_in_b, tab_s, tm)
    a_p, a_s = _attention(page_table, lams, subln, qx, kn, vt, qa_s, ka_s, va_s, kt_pages, v_pages, batch, seq,
                          dec_seq, lam_init)

    as_seq = lambda x: x.reshape(batch, seq, x.shape[-1])
    r_p, s_p = _ret_prompt(log_gamma, as_seq(qr_p), as_seq(kr_p), as_seq(vr_p), as_seq(gr_p), batch, seq)
    y_p = _merge_ffn(hp, a_p, r_p.reshape(batch * seq, RET_V_W), ga_p, gb_p, *weights, tm)

    r_s, s_s = _ret_sample(log_gamma, state_ret[l], qr_s, kr_s, vr_s, gr_s, dec_seq)
    y_s = _merge_ffn(hs, a_s, r_s, ga_s, gb_s, *weights, tm)

    k_p = jnp.transpose(kt_p.reshape(1, batch, DIFF_HEADS, 2, DIFF_HEAD_DIM, seq), (0, 1, 5, 2, 3, 4))
    return (
        y_p.reshape(batch, seq, D_MODEL),
        y_s.reshape(dec_batch, dec_seq, D_MODEL),
        k_p,
        v_p.reshape(1, batch, seq, DIFF_HEADS, DIFF_V_DIM),
        s_p[None],
        ka_s.reshape(1, dec_batch, dec_seq, DIFF_HEADS, 2, DIFF_HEAD_DIM),
        va_s.reshape(1, dec_batch, dec_seq, DIFF_HEADS, DIFF_V_DIM),
        s_s[None],
    )
```

```python
import functools
import math

import jax
import jax.numpy as jnp
import numpy as np
from jax import lax
from jax.experimental import pallas as pl
from jax.experimental.pallas import tpu as pltpu

F32 = jnp.float32
BF16 = jnp.bfloat16

D_MODEL = 1024
DIFF_HEADS = 8
DIFF_HEAD_DIM = 64
DIFF_V_DIM = 2 * DIFF_HEAD_DIM
ROT_DIM = DIFF_HEAD_DIM // 4
ROPE_THETA = 500000.0
RET_HEADS = 4
RET_KEY_DIM = 256
RET_VALUE_DIM = 512
RET_THETA = 10000.0
RET_CHUNK = 128
FFN_HIDDEN = 2816
NORM_EPS = 1e-5
NEG_INF = -1e30

DIFF_QK_W = DIFF_HEADS * 2 * DIFF_HEAD_DIM
DIFF_V_W = DIFF_HEADS * DIFF_V_DIM
RET_QK_W = RET_HEADS * RET_KEY_DIM
RET_V_W = RET_HEADS * RET_VALUE_DIM
IN_W = 3 * DIFF_QK_W + 2 * RET_QK_W + 2 * RET_V_W + 2 * D_MODEL

LANES = 128
ROPE_TAB_W = 3 * LANES
RET_TAB_W = 3 * RET_KEY_DIM
TAB_W = ROPE_TAB_W + RET_TAB_W
VMEM_LIMIT = 56 * 1024 * 1024
BF16_SUBLANES = 16
ACC_ROWS = DIFF_V_DIM + BF16_SUBLANES
LOG2E = math.log2(math.e)


def _lam_init(layer):
    return 0.8 - 0.6 * math.exp(-0.3 * layer)


def _sigmoid(x):
    return 1.0 / (1.0 + jnp.exp(-x))


def _rms(x):
    return x * lax.rsqrt(jnp.mean(x * x, axis=-1, keepdims=True) + NORM_EPS)


def _resident(shape):
    return pl.BlockSpec(shape, lambda *_: (0,) * len(shape), pipeline_mode=pl.Buffered(1))


def _params(*semantics):
    return pltpu.CompilerParams(dimension_semantics=semantics, vmem_limit_bytes=VMEM_LIMIT)


def _rotation_tables(pos):
    half = ROT_DIM // 2
    freqs = jnp.exp(-math.log(ROPE_THETA) * jnp.arange(half, dtype=F32) * (2.0 / ROT_DIM))
    ang = pos[:, None] * freqs[None, :]
    cos, sin = jnp.cos(ang), jnp.sin(ang)
    lane = np.arange(LANES) % DIFF_HEAD_DIM
    idx = lane % half
    rope_c = jnp.where(lane < ROT_DIM, cos[:, idx], 1.0)
    rope_s1 = jnp.where(lane < half, -sin[:, idx], 0.0)
    rope_s2 = jnp.where((lane >= half) & (lane < ROT_DIM), sin[:, idx], 0.0)

    rhalf = RET_KEY_DIM // 2
    angle = jnp.exp(-math.log(RET_THETA) * jnp.linspace(0.0, 1.0, rhalf, dtype=F32))
    rang = pos[:, None] * angle[None, :]
    rcos, rsin = jnp.cos(rang), jnp.sin(rang)
    lane2 = np.arange(RET_KEY_DIM)
    pair = lane2 // 2
    even = lane2 % 2 == 0
    ret_c = rcos[:, pair]
    ret_s1 = jnp.where(even, -rsin[:, pair], 0.0)
    ret_s2 = jnp.where(~even, rsin[:, pair], 0.0)
    return jnp.concatenate([rope_c, rope_s1, rope_s2, ret_c, ret_s1, ret_s2], axis=1).astype(F32)


def _partial_rope(acc, tab):
    c, s1, s2 = tab[:, 0:LANES], tab[:, LANES:2 * LANES], tab[:, 2 * LANES:3 * LANES]
    half = ROT_DIM // 2
    outs = []
    for g in range(acc.shape[1] // LANES):
        x = acc[:, g * LANES:(g + 1) * LANES]
        outs.append(x * c + pltpu.roll(x, LANES - half, 1) * s1 + pltpu.roll(x, half, 1) * s2)
    return jnp.concatenate(outs, axis=1)


def _ret_rotate(acc, tab):
    per_head = RET_KEY_DIM // LANES
    outs = []
    for g in range(acc.shape[1] // LANES):
        o = (g % per_head) * LANES
        c = tab[:, o:o + LANES]
        s1 = tab[:, RET_KEY_DIM + o:RET_KEY_DIM + o + LANES]
        s2 = tab[:, 2 * RET_KEY_DIM + o:2 * RET_KEY_DIM + o + LANES]
        x = acc[:, g * LANES:(g + 1) * LANES]
        outs.append(x * c + pltpu.roll(x, LANES - 1, 1) * s1 + pltpu.roll(x, 1, 1) * s2)
    return jnp.concatenate(outs, axis=1)


class _Projector:
    def __init__(self, x_ref, g_ref, w_ref, tab_ref, xn_ref):
        xn_ref[...] = (_rms(x_ref[...]) * g_ref[...]).astype(BF16)
        self._xn_ref, self._w_ref = xn_ref, w_ref
        self._rope_tab = tab_ref[:, 0:ROPE_TAB_W]
        self._ret_tab = tab_ref[:, ROPE_TAB_W:TAB_W]
        self._col = 0

    def _mm(self, n):
        c0, self._col = self._col, self._col + n
        return jnp.dot(self._xn_ref[...], self._w_ref[:, c0:c0 + n], preferred_element_type=F32)

    def qa(self):
        return _partial_rope(self._mm(DIFF_QK_W), self._rope_tab) * (DIFF_HEAD_DIM ** -0.5)

    def ka(self):
        return _partial_rope(self._mm(DIFF_QK_W), self._rope_tab)

    def va(self):
        return self._mm(DIFF_V_W)

    def qr(self):
        return _ret_rotate(self._mm(RET_QK_W), self._ret_tab)

    def kr(self):
        return _ret_rotate(self._mm(RET_QK_W), self._ret_tab) * (RET_KEY_DIM ** -0.5)

    def wide(self):
        return self._mm(RET_V_W)

    def gate(self):
        return self._mm(D_MODEL)


def _proj_sample_kernel(x_ref, g_ref, w_ref, tab_ref, qa_ref, ka_ref, va_ref, qr_ref, kr_ref, vr_ref, gr_ref, ga_ref,
                        gb_ref, xn_ref):
    p = _Projector(x_ref, g_ref, w_ref, tab_ref, xn_ref)
    qa_ref[...] = p.qa()
    ka_ref[...] = p.ka()
    va_ref[...] = p.va()
    qr_ref[...] = p.qr()
    kr_ref[...] = p.kr()
    vr_ref[...] = p.wide()
    gr_ref[...] = p.wide()
    ga_ref[...] = p.gate()
    gb_ref[...] = p.gate()


def _proj_prompt_kernel(x_ref, g_ref, w_ref, tab_ref, qx_ref, kn_ref, kt_ref, vt_ref, vo_ref, qr_ref, kr_ref, vr_ref,
                        gr_ref, ga_ref, gb_ref, xn_ref):
    p = _Projector(x_ref, g_ref, w_ref, tab_ref, xn_ref)
    tm = x_ref.shape[0]
    qt = (p.qa() * LOG2E).T.astype(BF16)
    zeros = jnp.zeros((DIFF_HEAD_DIM, tm), BF16)
    for hc in range(2 * DIFF_HEADS):
        comp = hc % 2
        src = qt[hc * DIFF_HEAD_DIM:(hc + 1) * DIFF_HEAD_DIM]
        lo, hi = (src, zeros) if comp == 0 else (zeros, src)
        qx_ref[hc * DIFF_V_DIM:hc * DIFF_V_DIM + DIFF_HEAD_DIM, :] = lo
        qx_ref[hc * DIFF_V_DIM + DIFF_HEAD_DIM:(hc + 1) * DIFF_V_DIM, :] = hi
    k = p.ka()
    kn_ref[...] = k.astype(BF16)
    kt_ref[...] = k.T
    v = p.va()
    vt_ref[...] = v.T.astype(BF16)
    for h in range(DIFF_HEADS):
        vo_ref[pl.ds(h, tm, stride=DIFF_HEADS), :] = v[:, h * DIFF_V_DIM:(h + 1) * DIFF_V_DIM]
    qr_ref[...] = p.qr().astype(BF16)
    kr_ref[...] = p.kr().astype(BF16)
    vr_ref[...] = p.wide().astype(BF16)
    gr_ref[...] = p.wide()
    ga_ref[...] = p.gate()
    gb_ref[...] = p.gate()


def _proj_in_specs(tm, tab_blocks):
    return [
        pl.BlockSpec((tm, D_MODEL), lambda i: (i, 0)),
        _resident((1, D_MODEL)),
        _resident((D_MODEL, IN_W)),
        pl.BlockSpec((tm, TAB_W), lambda i: (i % tab_blocks, 0)),
    ]


def _project_sample(x2d, gain, w_bf16, tab, tm):
    rows = x2d.shape[0]
    widths = (DIFF_QK_W, DIFF_QK_W, DIFF_V_W, RET_QK_W, RET_QK_W, RET_V_W, RET_V_W, D_MODEL, D_MODEL)
    return pl.pallas_call(
        _proj_sample_kernel,
        grid=(rows // tm,),
        in_specs=_proj_in_specs(tm, tab.shape[0] // tm),
        out_specs=[pl.BlockSpec((tm, w), lambda i: (i, 0)) for w in widths],
        out_shape=[jax.ShapeDtypeStruct((rows, w), F32) for w in widths],
        scratch_shapes=[pltpu.VMEM((tm, D_MODEL), BF16)],
        compiler_params=_params("parallel"),
        name="proj_sample",
    )(x2d, gain.reshape(1, D_MODEL), w_bf16, tab)


def _project_prompt(x2d, gain, w_bf16, tab, batch, seq, tm):
    rows = batch * seq
    per_b = seq // tm
    row = lambda w: pl.BlockSpec((tm, w), lambda i: (i, 0))
    transposed = lambda h: pl.BlockSpec((None, h, tm), lambda i: (i // per_b, 0, i % per_b))
    specs = [
        (transposed(2 * DIFF_QK_W), (batch, 2 * DIFF_QK_W, seq), BF16),
        (row(DIFF_QK_W), (rows, DIFF_QK_W), BF16),
        (transposed(DIFF_QK_W), (batch, DIFF_QK_W, seq), F32),
        (transposed(DIFF_V_W), (batch, DIFF_V_W, seq), BF16),
        (pl.BlockSpec((tm * DIFF_HEADS, DIFF_V_DIM), lambda i: (i, 0)), (rows * DIFF_HEADS, DIFF_V_DIM), F32),
        (row(RET_QK_W), (rows, RET_QK_W), BF16),
        (row(RET_QK_W), (rows, RET_QK_W), BF16),
        (row(RET_V_W), (rows, RET_V_W), BF16),
        (row(RET_V_W), (rows, RET_V_W), F32),
        (row(D_MODEL), (rows, D_MODEL), F32),
        (row(D_MODEL), (rows, D_MODEL), F32),
    ]
    return pl.pallas_call(
        _proj_prompt_kernel,
        grid=(rows // tm,),
        in_specs=_proj_in_specs(tm, tab.shape[0] // tm),
        out_specs=[s for s, _, _ in specs],
        out_shape=[jax.ShapeDtypeStruct(shape, dtype) for _, shape, dtype in specs],
        scratch_shapes=[pltpu.VMEM((tm, D_MODEL), BF16)],
        compiler_params=_params("parallel"),
        name="proj_prompt",
    )(x2d, gain.reshape(1, D_MODEL), w_bf16, tab)


def _lambda(lam_ref, lam_init):
    lv = lam_ref[...]
    s1 = jnp.sum(lv[0:1] * lv[1:2], axis=-1, keepdims=True)
    s2 = jnp.sum(lv[2:3] * lv[3:4], axis=-1, keepdims=True)
    return jnp.exp(s1) - jnp.exp(s2) + lam_init


def _dot_nt(a, b):
    return lax.dot_general(a, b, (((1,), (1,)), ((), ())), preferred_element_type=F32)


def _sub_layer_norm(o, sub_ref, lam_init):
    return _rms(o) * sub_ref[...] * (1.0 - lam_init)


def _attn_kernel(pt_ref, qi_ref, ki_ref, lam_ref, sub_ref, qx_ref, k_ref, vt_ref, qs_ref, kn_ref, vn_ref, *rest,
                 n_pages, lam_init):
    del pt_ref
    kt_pages = rest[:n_pages]
    v_pages = rest[n_pages:2 * n_pages]
    o_ref, os_ref, m_ref, acc_ref, st_ref, mx_ref = rest[2 * n_pages:]
    qi = qi_ref[pl.program_id(1)]
    ki = ki_ref[pl.program_id(1)]
    n_stat = 2 * DIFF_HEADS
    tk = k_ref.shape[0]
    lam = _lambda(lam_ref, lam_init)

    @pl.when(ki == 0)
    def _():
        m_ref[...] = jnp.full(m_ref.shape, NEG_INF, F32)
        acc_ref[...] = jnp.zeros(acc_ref.shape, F32)

    def scores(i, on_diagonal):
        h = i // 2
        k_h = k_ref[:, h * DIFF_V_DIM:(h + 1) * DIFF_V_DIM]
        st = jnp.dot(k_h, qx_ref[i * DIFF_V_DIM:(i + 1) * DIFF_V_DIM, :], preferred_element_type=F32)
        if on_diagonal:
            key = lax.broadcasted_iota(jnp.int32, st.shape, 0)
            qry = lax.broadcasted_iota(jnp.int32, st.shape, 1)
            st = jnp.where(key <= qry, st, NEG_INF)
        st_ref[i % 2] = st
        mx_ref[i % 2] = jnp.max(st, axis=0, keepdims=True)

    def step(on_diagonal):
        ones = jnp.ones((ACC_ROWS - DIFF_V_DIM, tk), BF16)
        probs = []
        scores(0, on_diagonal)
        for i in range(n_stat):
            if i + 1 < n_stat:
                scores(i + 1, on_diagonal)
            h = i // 2
            vt_h = jnp.concatenate([vt_ref[h * DIFF_V_DIM:(h + 1) * DIFF_V_DIM, :], ones], axis=0)
            m_prev = m_ref[i:i + 1, :]
            m_new = jnp.maximum(m_prev, mx_ref[i % 2])
            p = jnp.exp2(st_ref[i % 2] - m_new).astype(BF16)
            acc_ref[i] = jnp.exp2(m_prev - m_new) * acc_ref[i] + jnp.dot(vt_h, p, preferred_element_type=F32)
            m_ref[i:i + 1, :] = m_new
            if i % 2 == 1:
                probs.append(_sample_probs(h, qs_ref, kn_ref, kt_pages))
            if i % 4 == 3:
                _sample_values(h - 1, probs[h - 1:h + 1], lam, sub_ref, vn_ref, v_pages, os_ref, lam_init)

    def normalized(i):
        return acc_ref[i, 0:DIFF_V_DIM, :] / acc_ref[i, DIFF_V_DIM:DIFF_V_DIM + 1, :]

    @pl.when(ki < qi)
    def _():
        step(False)

    @pl.when(ki == qi)
    def _():
        step(True)
        for h in range(DIFF_HEADS):
            ot = normalized(2 * h) - lam * normalized(2 * h + 1)
            o_ref[:, h * DIFF_V_DIM:(h + 1) * DIFF_V_DIM] = _sub_layer_norm(ot.T, sub_ref, lam_init).astype(o_ref.dtype)


def _attention(page_table, lams, subln, qx, kn, vt, qa_s, ka_s, va_s, kt_pages, v_pages, batch, seq, dec_seq, lam_init,
               blk=512):
    nb = seq // blk
    n_stat = 2 * DIFF_HEADS
    dec_batch, n_pages = page_table.shape
    page = kt_pages.shape[2]
    tiles = [(qi, ki) for qi in range(nb) for ki in range(qi + 1)]
    assert batch * len(tiles) >= dec_batch
    qi_of = jnp.asarray([qi for qi, _ in tiles], jnp.int32)
    ki_of = jnp.asarray([ki for _, ki in tiles], jnp.int32)

    def seq_of(b, t):
        return jnp.minimum(b * len(tiles) + t, dec_batch - 1)

    tok_map = lambda b, t, pt, qo, ko: (seq_of(b, t), 0)

    def page_spec(j):
        return pl.BlockSpec((None, DIFF_QK_W, page), lambda b, t, pt, qo, ko: (pt[seq_of(b, t) * n_pages + j], 0, 0))

    const = lambda shape: pl.BlockSpec(shape, lambda b, t, pt, qo, ko: (0,) * len(shape))
    return pl.pallas_call(
        functools.partial(_attn_kernel, n_pages=n_pages, lam_init=lam_init),
        grid_spec=pltpu.PrefetchScalarGridSpec(
            num_scalar_prefetch=3,
            grid=(batch, len(tiles)),
            in_specs=[
                const((4, DIFF_HEAD_DIM)),
                const((1, DIFF_V_DIM)),
                pl.BlockSpec((None, 2 * DIFF_QK_W, blk), lambda b, t, pt, qo, ko: (b, 0, qo[t])),
                pl.BlockSpec((blk, DIFF_QK_W), lambda b, t, pt, qo, ko: (b * nb + ko[t], 0)),
                pl.BlockSpec((None, DIFF_V_W, blk), lambda b, t, pt, qo, ko: (b, 0, ko[t])),
                pl.BlockSpec((dec_seq, DIFF_QK_W), tok_map),
                pl.BlockSpec((dec_seq, DIFF_QK_W), tok_map),
                pl.BlockSpec((dec_seq, DIFF_V_W), tok_map),
            ] + [page_spec(j) for j in range(n_pages)] * 2,
            out_specs=[
                pl.BlockSpec((blk, DIFF_V_W), lambda b, t, pt, qo, ko: (b * nb + qo[t], 0)),
                pl.BlockSpec((dec_seq, DIFF_V_W), tok_map),
            ],
            scratch_shapes=[
                pltpu.VMEM((n_stat, blk), F32),
                pltpu.VMEM((n_stat, ACC_ROWS, blk), F32),
                pltpu.VMEM((2, blk, blk), F32),
                pltpu.VMEM((2, 1, blk), F32),
            ],
        ),
        out_shape=[
            jax.ShapeDtypeStruct((batch * seq, DIFF_V_W), BF16),
            jax.ShapeDtypeStruct((dec_batch * dec_seq, DIFF_V_W), F32),
        ],
        compiler_params=_params("arbitrary", "arbitrary"),
        name="attention",
    )(page_table.reshape(-1), qi_of, ki_of, lams, subln, qx, kn, vt, qa_s, ka_s, va_s,
      *([kt_pages] * n_pages), *([v_pages] * n_pages))


def _sample_probs(h, q_ref, kn_ref, kt_pages):
    t = q_ref.shape[0]
    page = kt_pages[0].shape[1]
    hs = slice(h * DIFF_V_DIM, (h + 1) * DIFF_V_DIM)
    pad = jnp.zeros((page - t, DIFF_V_DIM), BF16)
    row = lax.broadcasted_iota(jnp.int32, (2 * t, page), 0) % t
    col = lax.broadcasted_iota(jnp.int32, (2 * t, page), 1)
    lane = lax.broadcasted_iota(jnp.int32, (t, DIFF_V_DIM), 1)
    qh = q_ref[:, hs].astype(BF16)
    zero = jnp.zeros_like(qh)
    q = jnp.concatenate([jnp.where(lane < DIFF_HEAD_DIM, qh, zero), jnp.where(lane >= DIFF_HEAD_DIM, qh, zero)], axis=0)
    tiles = []
    for j in range(0, len(kt_pages), 2):
        kt = jnp.concatenate([kp[hs, :] for kp in kt_pages[j:j + 2]], axis=1).astype(BF16)
        tiles.append(jnp.dot(q, kt, preferred_element_type=F32))
    k_new = jnp.concatenate([kn_ref[:, hs].astype(BF16), pad], axis=0)
    tiles.append(jnp.where(col <= row, _dot_nt(q, k_new), NEG_INF))
    s = jnp.concatenate(tiles, axis=1)
    p = jnp.exp(s - jnp.max(s, axis=1, keepdims=True))
    return p.astype(BF16), jnp.sum(p, axis=1, keepdims=True)


def _sample_values(h0, probs, lam, sub_ref, vn_ref, v_pages, o_ref, lam_init):
    (p0, l0), (p1, l1) = probs
    t = vn_ref.shape[0]
    page = v_pages[0].shape[0] // DIFF_HEADS
    n_pages = len(v_pages)
    p = jnp.concatenate([p0, p1], axis=0)
    pad = jnp.zeros((page - t, 2 * DIFF_V_DIM), BF16)
    v_new = jnp.concatenate([vn_ref[:, h0 * DIFF_V_DIM:(h0 + 2) * DIFF_V_DIM].astype(BF16), pad], axis=0)
    acc = jnp.dot(p[:, n_pages * page:], v_new, preferred_element_type=F32)
    for j, vp in enumerate(v_pages):
        v_pair = jnp.concatenate([vp[pl.ds(h0 + d, page, stride=DIFF_HEADS), :] for d in (0, 1)], axis=1).astype(BF16)
        acc += jnp.dot(p[:, j * page:(j + 1) * page], v_pair, preferred_element_type=F32)
    for d, l in ((0, l0), (1, l1)):
        o = acc[2 * t * d:2 * t * (d + 1), d * DIFF_V_DIM:(d + 1) * DIFF_V_DIM] / l
        hs = slice((h0 + d) * DIFF_V_DIM, (h0 + d + 1) * DIFF_V_DIM)
        o_ref[:, hs] = _sub_layer_norm(o[:t] - lam * o[t:], sub_ref, lam_init).astype(o_ref.dtype)


def _iota_f32(shape, axis):
    return lax.broadcasted_iota(jnp.int32, shape, axis).astype(F32)


def _ret_decays(c, dk, dv, lg):
    rel = _iota_f32((c, c), 0) - _iota_f32((c, c), 1)
    decay = jnp.where(rel >= 0, jnp.exp(lg * jnp.maximum(rel, 0.0)), 0.0)
    q_dec = jnp.exp((_iota_f32((c, dk), 0) + 1.0) * lg)
    k_dec = jnp.exp((c - 1.0 - _iota_f32((c, dk), 0)) * lg)
    s_dec = jnp.exp(jnp.full((1, dv), float(c), F32) * lg)
    return decay, q_dec, k_dec, s_dec


def _ret_chunk(s_prev, q, k, v, decays):
    decay, q_dec, k_dec, s_dec = decays
    qk = _dot_nt(q.astype(BF16), k.astype(BF16)) * decay
    o = jnp.dot(qk.astype(BF16), v, preferred_element_type=F32)
    o = o + jnp.dot((q * q_dec).astype(BF16), s_prev.astype(BF16), preferred_element_type=F32)
    kv = jnp.dot((k * k_dec).T.astype(BF16), v, preferred_element_type=F32)
    return s_dec * s_prev + kv, o


def _gated_ret_out(o, gr):
    return _rms(o) * (gr * _sigmoid(gr))


def _ret_prompt_kernel(lg_ref, q_ref, k_ref, v_ref, gr_ref, r_ref, state_ref, s_ref):
    c = pl.program_id(1)

    @pl.when(c == 0)
    def _():
        s_ref[...] = jnp.zeros(s_ref.shape, F32)

    decay, q_dec, k_dec, s_dec = _ret_decays(RET_CHUNK, RET_KEY_DIM, RET_VALUE_DIM, lg_ref[pl.program_id(0)])
    batch = range(q_ref.shape[0])
    qk = [(_dot_nt(q_ref[b], k_ref[b]) * decay).astype(BF16) for b in batch]
    o = [jnp.dot(qk[b], v_ref[b], preferred_element_type=F32) for b in batch]
    for b in batch:
        qd = (q_ref[b].astype(F32) * q_dec).astype(BF16)
        o[b] = o[b] + jnp.dot(qd, s_ref[b].astype(BF16), preferred_element_type=F32)
    for b in batch:
        kd = (k_ref[b].astype(F32) * k_dec).T.astype(BF16)
        s_ref[b] = s_dec * s_ref[b] + jnp.dot(kd, v_ref[b], preferred_element_type=F32)
    for b in batch:
        r_ref[b] = _gated_ret_out(o[b], gr_ref[b]).astype(r_ref.dtype)

    @pl.when(c == pl.num_programs(1) - 1)
    def _():
        state_ref[...] = s_ref[...]


def _ret_prompt(log_gamma, qr, kr, vr, gr, batch, seq):
    tok = lambda w: pl.BlockSpec((batch, RET_CHUNK, w), lambda h, c: (0, c, h))
    return pl.pallas_call(
        _ret_prompt_kernel,
        grid=(RET_HEADS, seq // RET_CHUNK),
        in_specs=[pl.BlockSpec(memory_space=pltpu.SMEM), tok(RET_KEY_DIM), tok(RET_KEY_DIM), tok(RET_VALUE_DIM),
                  tok(RET_VALUE_DIM)],
        out_specs=[
            tok(RET_VALUE_DIM),
            pl.BlockSpec((batch, None, RET_KEY_DIM, RET_VALUE_DIM), lambda h, c: (0, h, 0, 0)),
        ],
        out_shape=[
            jax.ShapeDtypeStruct((batch, seq, RET_V_W), BF16),
            jax.ShapeDtypeStruct((batch, RET_HEADS, RET_KEY_DIM, RET_VALUE_DIM), F32),
        ],
        scratch_shapes=[pltpu.VMEM((batch, RET_KEY_DIM, RET_VALUE_DIM), F32)],
        compiler_params=_params("parallel", "arbitrary"),
        name="ret_prompt",
    )(log_gamma, qr, kr, vr, gr)


def _ret_sample_kernel(lg_ref, s_ref, q_ref, k_ref, v_ref, gr_ref, r_ref, state_ref):
    nb = s_ref.shape[0]
    t = q_ref.shape[0] // nb
    for h in range(RET_HEADS):
        ks = slice(h * RET_KEY_DIM, (h + 1) * RET_KEY_DIM)
        vs = slice(h * RET_VALUE_DIM, (h + 1) * RET_VALUE_DIM)
        decays = _ret_decays(t, RET_KEY_DIM, RET_VALUE_DIM, lg_ref[h])
        for b in range(nb):
            rows = slice(b * t, (b + 1) * t)
            s_new, o = _ret_chunk(s_ref[b, h], q_ref[rows, ks], k_ref[rows, ks], v_ref[rows, vs].astype(BF16), decays)
            state_ref[b, h] = s_new
            r_ref[rows, vs] = _gated_ret_out(o, gr_ref[rows, vs]).astype(r_ref.dtype)


def _ret_sample(log_gamma, state, qr, kr, vr, gr, dec_seq, nb=4):
    dec_batch = state.shape[0]
    tok_map = lambda b: (b, 0)
    state_spec = pl.BlockSpec((nb, RET_HEADS, RET_KEY_DIM, RET_VALUE_DIM), lambda b: (b, 0, 0, 0))
    return pl.pallas_call(
        _ret_sample_kernel,
        grid=(dec_batch // nb,),
        in_specs=[
            pl.BlockSpec(memory_space=pltpu.SMEM),
            state_spec,
            pl.BlockSpec((nb * dec_seq, RET_QK_W), tok_map),
            pl.BlockSpec((nb * dec_seq, RET_QK_W), tok_map),
            pl.BlockSpec((nb * dec_seq, RET_V_W), tok_map),
            pl.BlockSpec((nb * dec_seq, RET_V_W), tok_map),
        ],
        out_specs=[pl.BlockSpec((nb * dec_seq, RET_V_W), tok_map), state_spec],
        out_shape=[
            jax.ShapeDtypeStruct((dec_batch * dec_seq, RET_V_W), F32),
            jax.ShapeDtypeStruct(state.shape, F32),
        ],
        compiler_params=_params("parallel"),
        name="ret_sample",
    )(log_gamma, state, qr, kr, vr, gr)


def _merge_ffn_kernel(h_ref, a_ref, r_ref, ga_ref, gb_ref, bg_ref, wpa_ref, wpb_ref, wo_ref, gffn_ref, wgu_ref,
                      wdn_ref, gfin_ref, y_ref):
    pa = jnp.dot(a_ref[...].astype(BF16), wpa_ref[...], preferred_element_type=F32)
    pb = jnp.dot(r_ref[...].astype(BF16), wpb_ref[...], preferred_element_type=F32)
    m = (_sigmoid(ga_ref[...] + bg_ref[:, :D_MODEL]) * pa + _sigmoid(gb_ref[...] + bg_ref[:, D_MODEL:]) * pb)
    h1 = h_ref[...] + jnp.dot(m.astype(BF16), wo_ref[...], preferred_element_type=F32)
    xn = (_rms(h1) * gffn_ref[...]).astype(BF16)
    gu = jnp.dot(xn, wgu_ref[...], preferred_element_type=F32)
    g, u = gu[:, :FFN_HIDDEN], gu[:, FFN_HIDDEN:]
    hid = (g * _sigmoid(g) * u).astype(BF16)
    h2 = h1 + jnp.dot(hid, wdn_ref[...], preferred_element_type=F32)
    y_ref[...] = _rms(h2) * gfin_ref[...]


def _merge_ffn(h2d, a, r, ga, gb, b_gate, w_pa, w_pb, w_o, g_ffn, w_gu, w_dn, g_fin, tm):
    rows = h2d.shape[0]
    row = lambda w: pl.BlockSpec((tm, w), lambda i: (i, 0))
    return pl.pallas_call(
        _merge_ffn_kernel,
        grid=(rows // tm,),
        in_specs=[
            row(D_MODEL), row(DIFF_V_W), row(RET_V_W), row(D_MODEL), row(D_MODEL),
            _resident((1, 2 * D_MODEL)),
            _resident((DIFF_V_W, D_MODEL)), _resident((RET_V_W, D_MODEL)), _resident((D_MODEL, D_MODEL)),
            _resident((1, D_MODEL)),
            _resident((D_MODEL, 2 * FFN_HIDDEN)), _resident((FFN_HIDDEN, D_MODEL)),
            _resident((1, D_MODEL)),
        ],
        out_specs=row(D_MODEL),
        out_shape=jax.ShapeDtypeStruct((rows, D_MODEL), F32),
        compiler_params=_params("parallel"),
        name="merge_ffn",
    )(h2d, a, r, ga, gb, b_gate.reshape(1, -1), w_pa, w_pb, w_o, g_ffn.reshape(1, -1), w_gu, w_dn,
      g_fin.reshape(1, -1))


def kernel(x_prompt, x_sample, cache_k, cache_v, state_ret, page_table, norm_mix_g, w_in, b_gate, lambda_q1, lambda_k1,
           lambda_q2, lambda_k2, subln_g, w_pa, w_pb, w_o, norm_ffn_g, w_gu, w_down, norm_final_g):
    batch, seq, _ = x_prompt.shape
    dec_batch, dec_seq, _ = x_sample.shape
    depth, n_phys, page = cache_k.shape[:3]
    assert depth == 1, "the final norm is fused into the (single) layer's FFN kernel"
    past = page_table.shape[1] * page
    tm = 256

    log_gamma = jnp.log(1.0 - jnp.exp2(-5.0 - jnp.arange(RET_HEADS, dtype=F32)))
    tab_p = _rotation_tables(jnp.arange(seq, dtype=F32))
    tab_s = jnp.tile(_rotation_tables(jnp.arange(dec_seq, dtype=F32) + past), (tm // dec_seq, 1))

    l = 0
    lam_init = _lam_init(l)
    lams = jnp.stack([lambda_q1[l], lambda_k1[l], lambda_q2[l], lambda_k2[l]]).astype(F32)
    subln = subln_g[l].reshape(1, DIFF_V_DIM).astype(F32)
    w_in_b = w_in[l].astype(BF16)
    weights = (b_gate[l], w_pa[l].astype(BF16), w_pb[l].astype(BF16), w_o[l].astype(BF16), norm_ffn_g[l],
               w_gu[l].astype(BF16), w_down[l].astype(BF16), norm_final_g)
    kt_pages = jnp.transpose(cache_k[l], (0, 2, 3, 4, 1)).reshape(n_phys, DIFF_QK_W, page)
    v_pages = cache_v[l].reshape(n_phys, page * DIFF_HEADS, DIFF_V_DIM)

    hp = x_prompt.reshape(batch * seq, D_MODEL)
    hs = x_sample.reshape(dec_batch * dec_seq, D_MODEL)
    qx, kn, kt_p, vt, v_p, qr_p, kr_p, vr_p, gr_p, ga_p, gb_p = _project_prompt(
        hp, norm_mix_g[l], w_in_b, tab_p, batch, seq, tm)
    qa_s, ka_s, va_s, qr_s, kr_s, vr_s, gr_s, ga_s, gb_s = _project_sample(hs, norm_mix_g[l], w_in_b, tab_s, tm)
    a_p, a_s = _attention(page_table, lams, subln, qx, kn, vt, qa_s, ka_s, va_s, kt_pages, v_pages, batch, seq,
                          dec_seq, lam_init)

    as_seq = lambda x: x.reshape(batch, seq, x.shape[-1])
    r_p, s_p = _ret_prompt(log_gamma, as_seq(qr_p), as_seq(kr_p), as_seq(vr_p), as_seq(gr_p), batch, seq)
    y_p = _merge_ffn(hp, a_p, r_p.reshape(batch * seq, RET_V_W), ga_p, gb_p, *weights, tm)

    r_s, s_s = _ret_sample(log_gamma, state_ret[l], qr_s, kr_s, vr_s, gr_s, dec_seq)
    y_s = _merge_ffn(hs, a_s, r_s, ga_s, gb_s, *weights, tm)

    k_p = jnp.transpose(kt_p.reshape(1, batch, DIFF_HEADS, 2, DIFF_HEAD_DIM, seq), (0, 1, 5, 2, 3, 4))
    return (
        y_p.reshape(batch, seq, D_MODEL),
        y_s.reshape(dec_batch, dec_seq, D_MODEL),
        k_p,
        v_p.reshape(1, batch, seq, DIFF_HEADS, DIFF_V_DIM),
        s_p[None],
        ka_s.reshape(1, dec_batch, dec_seq, DIFF_HEADS, 2, DIFF_HEAD_DIM),
        va_s.reshape(1, dec_batch, dec_seq, DIFF_HEADS, DIFF_V_DIM),
        s_s[None],
    )
```

```python
import functools
import math

import jax
import jax.numpy as jnp
import numpy as np
from jax import lax
from jax.experimental import pallas as pl
from jax.experimental.pallas import tpu as pltpu

F32 = jnp.float32
BF16 = jnp.bfloat16

D_MODEL = 1024
DIFF_HEADS = 8
DIFF_HEAD_DIM = 64
DIFF_V_DIM = 2 * DIFF_HEAD_DIM
ROT_DIM = DIFF_HEAD_DIM // 4
ROPE_THETA = 500000.0
RET_HEADS = 4
RET_KEY_DIM = 256
RET_VALUE_DIM = 512
RET_THETA = 10000.0
RET_CHUNK = 128
FFN_HIDDEN = 2816
NORM_EPS = 1e-5
NEG_INF = -1e30

DIFF_QK_W = DIFF_HEADS * 2 * DIFF_HEAD_DIM
DIFF_V_W = DIFF_HEADS * DIFF_V_DIM
RET_QK_W = RET_HEADS * RET_KEY_DIM
RET_V_W = RET_HEADS * RET_VALUE_DIM
IN_W = 3 * DIFF_QK_W + 2 * RET_QK_W + 2 * RET_V_W + 2 * D_MODEL

LANES = 128
ROPE_TAB_W = 3 * LANES
RET_TAB_W = 3 * RET_KEY_DIM
TAB_W = ROPE_TAB_W + RET_TAB_W
VMEM_LIMIT = 56 * 1024 * 1024
BF16_SUBLANES = 16
ACC_ROWS = DIFF_V_DIM + BF16_SUBLANES
LOG2E = math.log2(math.e)


def _lam_init(layer):
    return 0.8 - 0.6 * math.exp(-0.3 * layer)


def _sigmoid(x):
    return 1.0 / (1.0 + jnp.exp(-x))


def _rms(x):
    return x * lax.rsqrt(jnp.mean(x * x, axis=-1, keepdims=True) + NORM_EPS)


def _resident(shape):
    return pl.BlockSpec(shape, lambda *_: (0,) * len(shape), pipeline_mode=pl.Buffered(1))


def _params(*semantics):
    return pltpu.CompilerParams(dimension_semantics=semantics, vmem_limit_bytes=VMEM_LIMIT)


def _rotation_tables(pos):
    half = ROT_DIM // 2
    freqs = jnp.exp(-math.log(ROPE_THETA) * jnp.arange(half, dtype=F32) * (2.0 / ROT_DIM))
    ang = pos[:, None] * freqs[None, :]
    cos, sin = jnp.cos(ang), jnp.sin(ang)
    lane = np.arange(LANES) % DIFF_HEAD_DIM
    idx = lane % half
    rope_c = jnp.where(lane < ROT_DIM, cos[:, idx], 1.0)
    rope_s1 = jnp.where(lane < half, -sin[:, idx], 0.0)
    rope_s2 = jnp.where((lane >= half) & (lane < ROT_DIM), sin[:, idx], 0.0)

    rhalf = RET_KEY_DIM // 2
    angle = jnp.exp(-math.log(RET_THETA) * jnp.linspace(0.0, 1.0, rhalf, dtype=F32))
    rang = pos[:, None] * angle[None, :]
    rcos, rsin = jnp.cos(rang), jnp.sin(rang)
    lane2 = np.arange(RET_KEY_DIM)
    pair = lane2 // 2
    even = lane2 % 2 == 0
    ret_c = rcos[:, pair]
    ret_s1 = jnp.where(even, -rsin[:, pair], 0.0)
    ret_s2 = jnp.where(~even, rsin[:, pair], 0.0)
    return jnp.concatenate([rope_c, rope_s1, rope_s2, ret_c, ret_s1, ret_s2], axis=1).astype(F32)


def _partial_rope(acc, tab):
    c, s1, s2 = tab[:, 0:LANES], tab[:, LANES:2 * LANES], tab[:, 2 * LANES:3 * LANES]
    half = ROT_DIM // 2
    outs = []
    for g in range(acc.shape[1] // LANES):
        x = acc[:, g * LANES:(g + 1) * LANES]
        outs.append(x * c + pltpu.roll(x, LANES - half, 1) * s1 + pltpu.roll(x, half, 1) * s2)
    return jnp.concatenate(outs, axis=1)


def _ret_rotate(acc, tab):
    per_head = RET_KEY_DIM // LANES
    outs = []
    for g in range(acc.shape[1] // LANES):
        o = (g % per_head) * LANES
        c = tab[:, o:o + LANES]
        s1 = tab[:, RET_KEY_DIM + o:RET_KEY_DIM + o + LANES]
        s2 = tab[:, 2 * RET_KEY_DIM + o:2 * RET_KEY_DIM + o + LANES]
        x = acc[:, g * LANES:(g + 1) * LANES]
        outs.append(x * c + pltpu.roll(x, LANES - 1, 1) * s1 + pltpu.roll(x, 1, 1) * s2)
    return jnp.concatenate(outs, axis=1)


class _Projector:
    def __init__(self, x_ref, g_ref, w_ref, tab_ref, xn_ref):
        xn_ref[...] = (_rms(x_ref[...]) * g_ref[...]).astype(BF16)
        self._xn_ref, self._w_ref = xn_ref, w_ref
        self._rope_tab = tab_ref[:, 0:ROPE_TAB_W]
        self._ret_tab = tab_ref[:, ROPE_TAB_W:TAB_W]
        self._col = 0

    def _mm(self, n):
        c0, self._col = self._col, self._col + n
        return jnp.dot(self._xn_ref[...], self._w_ref[:, c0:c0 + n], preferred_element_type=F32)

    def qa(self):
        return _partial_rope(self._mm(DIFF_QK_W), self._rope_tab) * (DIFF_HEAD_DIM ** -0.5)

    def ka(self):
        return _partial_rope(self._mm(DIFF_QK_W), self._rope_tab)

    def va(self):
        return self._mm(DIFF_V_W)

    def qr(self):
        return _ret_rotate(self._mm(RET_QK_W), self._ret_tab)

    def kr(self):
        return _ret_rotate(self._mm(RET_QK_W), self._ret_tab) * (RET_KEY_DIM ** -0.5)

    def wide(self):
        return self._mm(RET_V_W)

    def gate(self):
        return self._mm(D_MODEL)


def _proj_sample_kernel(x_ref, g_ref, w_ref, tab_ref, qa_ref, ka_ref, va_ref, qr_ref, kr_ref, vr_ref, gr_ref, ga_ref,
                        gb_ref, xn_ref):
    p = _Projector(x_ref, g_ref, w_ref, tab_ref, xn_ref)
    qa_ref[...] = p.qa()
    ka_ref[...] = p.ka()
    va_ref[...] = p.va()
    qr_ref[...] = p.qr()
    kr_ref[...] = p.kr()
    vr_ref[...] = p.wide()
    gr_ref[...] = p.wide()
    ga_ref[...] = p.gate()
    gb_ref[...] = p.gate()


def _proj_prompt_kernel(x_ref, g_ref, w_ref, tab_ref, qx_ref, kn_ref, kt_ref, vt_ref, vo_ref, qr_ref, kr_ref, vr_ref,
                        gr_ref, ga_ref, gb_ref, xn_ref):
    p = _Projector(x_ref, g_ref, w_ref, tab_ref, xn_ref)
    tm = x_ref.shape[0]
    qt = (p.qa() * LOG2E).T.astype(BF16)
    zeros = jnp.zeros((DIFF_HEAD_DIM, tm), BF16)
    for hc in range(2 * DIFF_HEADS):
        comp = hc % 2
        src = qt[hc * DIFF_HEAD_DIM:(hc + 1) * DIFF_HEAD_DIM]
        lo, hi = (src, zeros) if comp == 0 else (zeros, src)
        qx_ref[hc * DIFF_V_DIM:hc * DIFF_V_DIM + DIFF_HEAD_DIM, :] = lo
        qx_ref[hc * DIFF_V_DIM + DIFF_HEAD_DIM:(hc + 1) * DIFF_V_DIM, :] = hi
    k = p.ka()
    kn_ref[...] = k.astype(BF16)
    kt_ref[...] = k.T
    v = p.va()
    vt_ref[...] = v.T.astype(BF16)
    for h in range(DIFF_HEADS):
        vo_ref[pl.ds(h, tm, stride=DIFF_HEADS), :] = v[:, h * DIFF_V_DIM:(h + 1) * DIFF_V_DIM]
    qr_ref[...] = p.qr().astype(BF16)
    kr_ref[...] = p.kr().astype(BF16)
    vr_ref[...] = p.wide().astype(BF16)
    gr_ref[...] = p.wide()
    ga_ref[...] = p.gate()
    gb_ref[...] = p.gate()


def _proj_in_specs(tm, tab_blocks):
    return [
        pl.BlockSpec((tm, D_MODEL), lambda i: (i, 0)),
        _resident((1, D_MODEL)),
        _resident((D_MODEL, IN_W)),
        pl.BlockSpec((tm, TAB_W), lambda i: (i % tab_blocks, 0)),
    ]


def _project_sample(x2d, gain, w_bf16, tab, tm):
    rows = x2d.shape[0]
    widths = (DIFF_QK_W, DIFF_QK_W, DIFF_V_W, RET_QK_W, RET_QK_W, RET_V_W, RET_V_W, D_MODEL, D_MODEL)
    return pl.pallas_call(
        _proj_sample_kernel,
        grid=(rows // tm,),
        in_specs=_proj_in_specs(tm, tab.shape[0] // tm),
        out_specs=[pl.BlockSpec((tm, w), lambda i: (i, 0)) for w in widths],
        out_shape=[jax.ShapeDtypeStruct((rows, w), F32) for w in widths],
        scratch_shapes=[pltpu.VMEM((tm, D_MODEL), BF16)],
        compiler_params=_params("parallel"),
        name="proj_sample",
    )(x2d, gain.reshape(1, D_MODEL), w_bf16, tab)


def _project_prompt(x2d, gain, w_bf16, tab, batch, seq, tm):
    rows = batch * seq
    per_b = seq // tm
    row = lambda w: pl.BlockSpec((tm, w), lambda i: (i, 0))
    transposed = lambda h: pl.BlockSpec((None, h, tm), lambda i: (i // per_b, 0, i % per_b))
    specs = [
        (transposed(2 * DIFF_QK_W), (batch, 2 * DIFF_QK_W, seq), BF16),
        (row(DIFF_QK_W), (rows, DIFF_QK_W), BF16),
        (transposed(DIFF_QK_W), (batch, DIFF_QK_W, seq), F32),
        (transposed(DIFF_V_W), (batch, DIFF_V_W, seq), BF16),
        (pl.BlockSpec((tm * DIFF_HEADS, DIFF_V_DIM), lambda i: (i, 0)), (rows * DIFF_HEADS, DIFF_V_DIM), F32),
        (row(RET_QK_W), (rows, RET_QK_W), BF16),
        (row(RET_QK_W), (rows, RET_QK_W), BF16),
        (row(RET_V_W), (rows, RET_V_W), BF16),
        (row(RET_V_W), (rows, RET_V_W), F32),
        (row(D_MODEL), (rows, D_MODEL), F32),
        (row(D_MODEL), (rows, D_MODEL), F32),
    ]
    return pl.pallas_call(
        _proj_prompt_kernel,
        grid=(rows // tm,),
        in_specs=_proj_in_specs(tm, tab.shape[0] // tm),
        out_specs=[s for s, _, _ in specs],
        out_shape=[jax.ShapeDtypeStruct(shape, dtype) for _, shape, dtype in specs],
        scratch_shapes=[pltpu.VMEM((tm, D_MODEL), BF16)],
        compiler_params=_params("parallel"),
        name="proj_prompt",
    )(x2d, gain.reshape(1, D_MODEL), w_bf16, tab)


def _lambda(lam_ref, lam_init):
    lv = lam_ref[...]
    s1 = jnp.sum(lv[0:1] * lv[1:2], axis=-1, keepdims=True)
    s2 = jnp.sum(lv[2:3] * lv[3:4], axis=-1, keepdims=True)
    return jnp.exp(s1) - jnp.exp(s2) + lam_init


def _dot_nt(a, b):
    return lax.dot_general(a, b, (((1,), (1,)), ((), ())), preferred_element_type=F32)


def _sub_layer_norm(o, sub_ref, lam_init):
    return _rms(o) * sub_ref[...] * (1.0 - lam_init)


def _attn_kernel(pt_ref, qi_ref, ki_ref, lam_ref, sub_ref, qx_ref, k_ref, vt_ref, qs_ref, kn_ref, vn_ref, *rest,
                 n_pages, lam_init):
    del pt_ref
    kt_pages = rest[:n_pages]
    v_pages = rest[n_pages:2 * n_pages]
    o_ref, os_ref, m_ref, acc_ref, st_ref, mx_ref = rest[2 * n_pages:]
    qi = qi_ref[pl.program_id(1)]
    ki = ki_ref[pl.program_id(1)]
    n_stat = 2 * DIFF_HEADS
    tk = k_ref.shape[0]
    lam = _lambda(lam_ref, lam_init)

    @pl.when(ki == 0)
    def _():
        m_ref[...] = jnp.full(m_ref.shape, NEG_INF, F32)
        acc_ref[...] = jnp.zeros(acc_ref.shape, F32)

    def scores(i, on_diagonal):
        h = i // 2
        k_h = k_ref[:, h * DIFF_V_DIM:(h + 1) * DIFF_V_DIM]
        st = jnp.dot(k_h, qx_ref[i * DIFF_V_DIM:(i + 1) * DIFF_V_DIM, :], preferred_element_type=F32)
        if on_diagonal:
            key = lax.broadcasted_iota(jnp.int32, st.shape, 0)
            qry = lax.broadcasted_iota(jnp.int32, st.shape, 1)
            st = jnp.where(key <= qry, st, NEG_INF)
        st_ref[i % 2] = st
        mx_ref[i % 2] = jnp.max(st, axis=0, keepdims=True)

    def step(on_diagonal):
        ones = jnp.ones((ACC_ROWS - DIFF_V_DIM, tk), BF16)
        probs = []
        scores(0, on_diagonal)
        for i in range(n_stat):
            if i + 1 < n_stat:
                scores(i + 1, on_diagonal)
            h = i // 2
            vt_h = jnp.concatenate([vt_ref[h * DIFF_V_DIM:(h + 1) * DIFF_V_DIM, :], ones], axis=0)
            m_prev = m_ref[i:i + 1, :]
            m_new = jnp.maximum(m_prev, mx_ref[i % 2])
            p = jnp.exp2(st_ref[i % 2] - m_new).astype(BF16)
            acc_ref[i] = jnp.exp2(m_prev - m_new) * acc_ref[i] + jnp.dot(vt_h, p, preferred_element_type=F32)
            m_ref[i:i + 1, :] = m_new
            if i % 2 == 1:
                probs.append(_sample_probs(h, qs_ref, kn_ref, kt_pages))
            if i % 4 == 3:
                _sample_values(h - 1, probs[h - 1:h + 1], lam, sub_ref, vn_ref, v_pages, os_ref, lam_init)

    def normalized(i):
        return acc_ref[i, 0:DIFF_V_DIM, :] / acc_ref[i, DIFF_V_DIM:DIFF_V_DIM + 1, :]

    @pl.when(ki < qi)
    def _():
        step(False)

    @pl.when(ki == qi)
    def _():
        step(True)
        for h in range(DIFF_HEADS):
            ot = normalized(2 * h) - lam * normalized(2 * h + 1)
            o_ref[:, h * DIFF_V_DIM:(h + 1) * DIFF_V_DIM] = _sub_layer_norm(ot.T, sub_ref, lam_init).astype(o_ref.dtype)


def _attention(page_table, lams, subln, qx, kn, vt, qa_s, ka_s, va_s, kt_pages, v_pages, batch, seq, dec_seq, lam_init,
               blk=512):
    nb = seq // blk
    n_stat = 2 * DIFF_HEADS
    dec_batch, n_pages = page_table.shape
    page = kt_pages.shape[2]
    tiles = [(qi, ki) for qi in range(nb) for ki in range(qi + 1)]
    assert batch * len(tiles) >= dec_batch
    qi_of = jnp.asarray([qi for qi, _ in tiles], jnp.int32)
    ki_of = jnp.asarray([ki for _, ki in tiles], jnp.int32)

    def seq_of(b, t):
        return jnp.minimum(b * len(tiles) + t, dec_batch - 1)

    tok_map = lambda b, t, pt, qo, ko: (seq_of(b, t), 0)

    def page_spec(j):
        return pl.BlockSpec((None, DIFF_QK_W, page), lambda b, t, pt, qo, ko: (pt[seq_of(b, t) * n_pages + j], 0, 0))

    const = lambda shape: pl.BlockSpec(shape, lambda b, t, pt, qo, ko: (0,) * len(shape))
    return pl.pallas_call(
        functools.partial(_attn_kernel, n_pages=n_pages, lam_init=lam_init),
        grid_spec=pltpu.PrefetchScalarGridSpec(
            num_scalar_prefetch=3,
            grid=(batch, len(tiles)),
            in_specs=[
                const((4, DIFF_HEAD_DIM)),
                const((1, DIFF_V_DIM)),
                pl.BlockSpec((None, 2 * DIFF_QK_W, blk), lambda b, t, pt, qo, ko: (b, 0, qo[t])),
                pl.BlockSpec((blk, DIFF_QK_W), lambda b, t, pt, qo, ko: (b * nb + ko[t], 0)),
                pl.BlockSpec((None, DIFF_V_W, blk), lambda b, t, pt, qo, ko: (b, 0, ko[t])),
                pl.BlockSpec((dec_seq, DIFF_QK_W), tok_map),
                pl.BlockSpec((dec_seq, DIFF_QK_W), tok_map),
                pl.BlockSpec((dec_seq, DIFF_V_W), tok_map),
            ] + [page_spec(j) for j in range(n_pages)] * 2,
            out_specs=[
                pl.BlockSpec((blk, DIFF_V_W), lambda b, t, pt, qo, ko: (b * nb + qo[t], 0)),
                pl.BlockSpec((dec_seq, DIFF_V_W), tok_map),
            ],
            scratch_shapes=[
                pltpu.VMEM((n_stat, blk), F32),
                pltpu.VMEM((n_stat, ACC_ROWS, blk), F32),
                pltpu.VMEM((2, blk, blk), F32),
                pltpu.VMEM((2, 1, blk), F32),
            ],
        ),
        out_shape=[
            jax.ShapeDtypeStruct((batch * seq, DIFF_V_W), BF16),
            jax.ShapeDtypeStruct((dec_batch * dec_seq, DIFF_V_W), F32),
        ],
        compiler_params=_params("arbitrary", "arbitrary"),
        name="attention",
    )(page_table.reshape(-1), qi_of, ki_of, lams, subln, qx, kn, vt, qa_s, ka_s, va_s,
      *([kt_pages] * n_pages), *([v_pages] * n_pages))


def _sample_probs(h, q_ref, kn_ref, kt_pages):
    t = q_ref.shape[0]
    page = kt_pages[0].shape[1]
    hs = slice(h * DIFF_V_DIM, (h + 1) * DIFF_V_DIM)
    pad = jnp.zeros((page - t, DIFF_V_DIM), BF16)
    row = lax.broadcasted_iota(jnp.int32, (2 * t, page), 0) % t
    col = lax.broadcasted_iota(jnp.int32, (2 * t, page), 1)
    lane = lax.broadcasted_iota(jnp.int32, (t, DIFF_V_DIM), 1)
    qh = q_ref[:, hs].astype(BF16)
    zero = jnp.zeros_like(qh)
    q = jnp.concatenate([jnp.where(lane < DIFF_HEAD_DIM, qh, zero), jnp.where(lane >= DIFF_HEAD_DIM, qh, zero)], axis=0)
    tiles = []
    for j in range(0, len(kt_pages), 2):
        kt = jnp.concatenate([kp[hs, :] for kp in kt_pages[j:j + 2]], axis=1).astype(BF16)
        tiles.append(jnp.dot(q, kt, preferred_element_type=F32))
    k_new = jnp.concatenate([kn_ref[:, hs].astype(BF16), pad], axis=0)
    tiles.append(jnp.where(col <= row, _dot_nt(q, k_new), NEG_INF))
    s = jnp.concatenate(tiles, axis=1)
    p = jnp.exp(s - jnp.max(s, axis=1, keepdims=True))
    return p.astype(BF16), jnp.sum(p, axis=1, keepdims=True)


def _sample_values(h0, probs, lam, sub_ref, vn_ref, v_pages, o_ref, lam_init):
    (p0, l0), (p1, l1) = probs
    t = vn_ref.shape[0]
    page = v_pages[0].shape[0] // DIFF_HEADS
    n_pages = len(v_pages)
    p = jnp.concatenate([p0, p1], axis=0)
    pad = jnp.zeros((page - t, 2 * DIFF_V_DIM), BF16)
    v_new = jnp.concatenate([vn_ref[:, h0 * DIFF_V_DIM:(h0 + 2) * DIFF_V_DIM].astype(BF16), pad], axis=0)
    acc = jnp.dot(p[:, n_pages * page:], v_new, preferred_element_type=F32)
    for j, vp in enumerate(v_pages):
        v_pair = jnp.concatenate([vp[pl.ds(h0 + d, page, stride=DIFF_HEADS), :] for d in (0, 1)], axis=1).astype(BF16)
        acc += jnp.dot(p[:, j * page:(j + 1) * page], v_pair, preferred_element_type=F32)
    for d, l in ((0, l0), (1, l1)):
        o = acc[2 * t * d:2 * t * (d + 1), d * DIFF_V_DIM:(d + 1) * DIFF_V_DIM] / l
        hs = slice((h0 + d) * DIFF_V_DIM, (h0 + d + 1) * DIFF_V_DIM)
        o_ref[:, hs] = _sub_layer_norm(o[:t] - lam * o[t:], sub_ref, lam_init).astype(o_ref.dtype)


def _iota_f32(shape, axis):
    return lax.broadcasted_iota(jnp.int32, shape, axis).astype(F32)


def _ret_decays(c, dk, dv, lg):
    rel = _iota_f32((c, c), 0) - _iota_f32((c, c), 1)
    decay = jnp.where(rel >= 0, jnp.exp(lg * jnp.maximum(rel, 0.0)), 0.0)
    q_dec = jnp.exp((_iota_f32((c, dk), 0) + 1.0) * lg)
    k_dec = jnp.exp((c - 1.0 - _iota_f32((c, dk), 0)) * lg)
    s_dec = jnp.exp(jnp.full((1, dv), float(c), F32) * lg)
    return decay, q_dec, k_dec, s_dec


def _ret_chunk(s_prev, q, k, v, decays):
    decay, q_dec, k_dec, s_dec = decays
    qk = _dot_nt(q.astype(BF16), k.astype(BF16)) * decay
    o = jnp.dot(qk.astype(BF16), v, preferred_element_type=F32)
    o = o + jnp.dot((q * q_dec).astype(BF16), s_prev.astype(BF16), preferred_element_type=F32)
    kv = jnp.dot((k * k_dec).T.astype(BF16), v, preferred_element_type=F32)
    return s_dec * s_prev + kv, o


def _gated_ret_out(o, gr):
    return _rms(o) * (gr * _sigmoid(gr))


def _retention_kernel(lg_ref, q_ref, k_ref, v_ref, gr_ref, ss_ref, qs_ref, ks_ref, vs_ref, grs_ref, r_ref, state_ref,
                      rs_ref, sso_ref, s_ref):
    c = pl.program_id(1)

    @pl.when(c == 0)
    def _():
        s_ref[...] = jnp.zeros(s_ref.shape, F32)

    n_stages = 4

    def decode_heads(stage):
        t = qs_ref.shape[0]
        for h in range(stage, RET_HEADS, n_stages):
            ks = slice(h * RET_KEY_DIM, (h + 1) * RET_KEY_DIM)
            vs = slice(h * RET_VALUE_DIM, (h + 1) * RET_VALUE_DIM)
            decays = _ret_decays(t, RET_KEY_DIM, RET_VALUE_DIM, lg_ref[h])
            s_new, o = _ret_chunk(ss_ref[h], qs_ref[:, ks], ks_ref[:, ks], vs_ref[:, vs].astype(BF16), decays)
            sso_ref[h] = s_new
            rs_ref[:, vs] = _gated_ret_out(o, grs_ref[:, vs]).astype(rs_ref.dtype)

    decay, q_dec, k_dec, s_dec = _ret_decays(RET_CHUNK, RET_KEY_DIM, RET_VALUE_DIM, lg_ref[pl.program_id(0)])
    batch = range(q_ref.shape[0])
    qk = [(_dot_nt(q_ref[b], k_ref[b]) * decay).astype(BF16) for b in batch]
    o = [jnp.dot(qk[b], v_ref[b], preferred_element_type=F32) for b in batch]
    decode_heads(0)
    for b in batch:
        qd = (q_ref[b].astype(F32) * q_dec).astype(BF16)
        o[b] = o[b] + jnp.dot(qd, s_ref[b].astype(BF16), preferred_element_type=F32)
    decode_heads(1)
    for b in batch:
        kd = (k_ref[b].astype(F32) * k_dec).T.astype(BF16)
        s_ref[b] = s_dec * s_ref[b] + jnp.dot(kd, v_ref[b], preferred_element_type=F32)
    decode_heads(2)
    for b in batch:
        r_ref[b] = _gated_ret_out(o[b], gr_ref[b]).astype(r_ref.dtype)
    decode_heads(3)

    @pl.when(c == pl.num_programs(1) - 1)
    def _():
        state_ref[...] = s_ref[...]


def _retention(log_gamma, qr, kr, vr, gr, state_s, qr_s, kr_s, vr_s, gr_s, batch, seq, dec_seq):
    nc = seq // RET_CHUNK
    dec_batch = state_s.shape[0]
    assert RET_HEADS * nc >= dec_batch
    seq_of = lambda h, c: jnp.minimum(h * nc + c, dec_batch - 1)
    tok = lambda w: pl.BlockSpec((batch, RET_CHUNK, w), lambda h, c: (0, c, h))
    tok_s = lambda w: pl.BlockSpec((dec_seq, w), lambda h, c: (seq_of(h, c), 0))
    state_s_spec = pl.BlockSpec((None, RET_HEADS, RET_KEY_DIM, RET_VALUE_DIM), lambda h, c: (seq_of(h, c), 0, 0, 0))
    return pl.pallas_call(
        _retention_kernel,
        grid=(RET_HEADS, nc),
        in_specs=[pl.BlockSpec(memory_space=pltpu.SMEM), tok(RET_KEY_DIM), tok(RET_KEY_DIM), tok(RET_VALUE_DIM),
                  tok(RET_VALUE_DIM), state_s_spec, tok_s(RET_QK_W), tok_s(RET_QK_W), tok_s(RET_V_W), tok_s(RET_V_W)],
        out_specs=[
            tok(RET_VALUE_DIM),
            pl.BlockSpec((batch, None, RET_KEY_DIM, RET_VALUE_DIM), lambda h, c: (0, h, 0, 0)),
            tok_s(RET_V_W),
            state_s_spec,
        ],
        out_shape=[
            jax.ShapeDtypeStruct((batch, seq, RET_V_W), BF16),
            jax.ShapeDtypeStruct((batch, RET_HEADS, RET_KEY_DIM, RET_VALUE_DIM), F32),
            jax.ShapeDtypeStruct((dec_batch * dec_seq, RET_V_W), F32),
            jax.ShapeDtypeStruct(state_s.shape, F32),
        ],
        scratch_shapes=[pltpu.VMEM((batch, RET_KEY_DIM, RET_VALUE_DIM), F32)],
        compiler_params=_params("arbitrary", "arbitrary"),
        name="retention",
    )(log_gamma, qr, kr, vr, gr, state_s, qr_s, kr_s, vr_s, gr_s)


def _merge_ffn_kernel(h_ref, a_ref, r_ref, ga_ref, gb_ref, bg_ref, wpa_ref, wpb_ref, wo_ref, gffn_ref, wgu_ref,
                      wdn_ref, gfin_ref, y_ref):
    pa = jnp.dot(a_ref[...].astype(BF16), wpa_ref[...], preferred_element_type=F32)
    pb = jnp.dot(r_ref[...].astype(BF16), wpb_ref[...], preferred_element_type=F32)
    m = (_sigmoid(ga_ref[...] + bg_ref[:, :D_MODEL]) * pa + _sigmoid(gb_ref[...] + bg_ref[:, D_MODEL:]) * pb)
    h1 = h_ref[...] + jnp.dot(m.astype(BF16), wo_ref[...], preferred_element_type=F32)
    xn = (_rms(h1) * gffn_ref[...]).astype(BF16)
    gu = jnp.dot(xn, wgu_ref[...], preferred_element_type=F32)
    g, u = gu[:, :FFN_HIDDEN], gu[:, FFN_HIDDEN:]
    hid = (g * _sigmoid(g) * u).astype(BF16)
    h2 = h1 + jnp.dot(hid, wdn_ref[...], preferred_element_type=F32)
    y_ref[...] = _rms(h2) * gfin_ref[...]


def _merge_ffn(h2d, a, r, ga, gb, b_gate, w_pa, w_pb, w_o, g_ffn, w_gu, w_dn, g_fin, tm):
    rows = h2d.shape[0]
    row = lambda w: pl.BlockSpec((tm, w), lambda i: (i, 0))
    return pl.pallas_call(
        _merge_ffn_kernel,
        grid=(rows // tm,),
        in_specs=[
            row(D_MODEL), row(DIFF_V_W), row(RET_V_W), row(D_MODEL), row(D_MODEL),
            _resident((1, 2 * D_MODEL)),
            _resident((DIFF_V_W, D_MODEL)), _resident((RET_V_W, D_MODEL)), _resident((D_MODEL, D_MODEL)),
            _resident((1, D_MODEL)),
            _resident((D_MODEL, 2 * FFN_HIDDEN)), _resident((FFN_HIDDEN, D_MODEL)),
            _resident((1, D_MODEL)),
        ],
        out_specs=row(D_MODEL),
        out_shape=jax.ShapeDtypeStruct((rows, D_MODEL), F32),
        compiler_params=_params("parallel"),
        name="merge_ffn",
    )(h2d, a, r, ga, gb, b_gate.reshape(1, -1), w_pa, w_pb, w_o, g_ffn.reshape(1, -1), w_gu, w_dn,
      g_fin.reshape(1, -1))


def kernel(x_prompt, x_sample, cache_k, cache_v, state_ret, page_table, norm_mix_g, w_in, b_gate, lambda_q1, lambda_k1,
           lambda_q2, lambda_k2, subln_g, w_pa, w_pb, w_o, norm_ffn_g, w_gu, w_down, norm_final_g):
    batch, seq, _ = x_prompt.shape
    dec_batch, dec_seq, _ = x_sample.shape
    depth, n_phys, page = cache_k.shape[:3]
    assert depth == 1, "the final norm is fused into the (single) layer's FFN kernel"
    past = page_table.shape[1] * page
    tm = 256

    log_gamma = jnp.log(1.0 - jnp.exp2(-5.0 - jnp.arange(RET_HEADS, dtype=F32)))
    tab_p = _rotation_tables(jnp.arange(seq, dtype=F32))
    tab_s = jnp.tile(_rotation_tables(jnp.arange(dec_seq, dtype=F32) + past), (tm // dec_seq, 1))

    l = 0
    lam_init = _lam_init(l)
    lams = jnp.stack([lambda_q1[l], lambda_k1[l], lambda_q2[l], lambda_k2[l]]).astype(F32)
    subln = subln_g[l].reshape(1, DIFF_V_DIM).astype(F32)
    w_in_b = w_in[l].astype(BF16)
    weights = (b_gate[l], w_pa[l].astype(BF16), w_pb[l].astype(BF16), w_o[l].astype(BF16), norm_ffn_g[l],
               w_gu[l].astype(BF16), w_down[l].astype(BF16), norm_final_g)
    kt_pages = jnp.transpose(cache_k[l], (0, 2, 3, 4, 1)).reshape(n_phys, DIFF_QK_W, page)
    v_pages = cache_v[l].reshape(n_phys, page * DIFF_HEADS, DIFF_V_DIM)

    hp = x_prompt.reshape(batch * seq, D_MODEL)
    hs = x_sample.reshape(dec_batch * dec_seq, D_MODEL)
    qx, kn, kt_p, vt, v_p, qr_p, kr_p, vr_p, gr_p, ga_p, gb_p = _project_prompt(
        hp, norm_mix_g[l], w_in_b, tab_p, batch, seq, tm)
    qa_s, ka_s, va_s, qr_s, kr_s, vr_s, gr_s, ga_s, gb_s = _project_sample(hs, norm_mix_g[l], w_in_b, tab_s, tm)
    a_p, a_s = _attention(page_table, lams, subln, qx, kn, vt, qa_s, ka_s, va_s, kt_pages, v_pages, batch, seq,
                          dec_seq, lam_init)

    as_seq = lambda x: x.reshape(batch, seq, x.shape[-1])
    r_p, s_p, r_s, s_s = _retention(log_gamma, as_seq(qr_p), as_seq(kr_p), as_seq(vr_p), as_seq(gr_p), state_ret[l],
                                    qr_s, kr_s, vr_s, gr_s, batch, seq, dec_seq)
    y_p = _merge_ffn(hp, a_p, r_p.reshape(batch * seq, RET_V_W), ga_p, gb_p, *weights, tm)
    y_s = _merge_ffn(hs, a_s, r_s, ga_s, gb_s, *weights, tm)

    k_p = jnp.transpose(kt_p.reshape(1, batch, DIFF_HEADS, 2, DIFF_HEAD_DIM, seq), (0, 1, 5, 2, 3, 4))
    return (
        y_p.reshape(batch, seq, D_MODEL),
        y_s.reshape(dec_batch, dec_seq, D_MODEL),
        k_p,
        v_p.reshape(1, batch, seq, DIFF_HEADS, DIFF_V_DIM),
        s_p[None],
        ka_s.reshape(1, dec_batch, dec_seq, DIFF_HEADS, 2, DIFF_HEAD_DIM),
        va_s.reshape(1, dec_batch, dec_seq, DIFF_HEADS, DIFF_V_DIM),
        s_s[None],
    )
```

```python
import functools
import math

import jax
import jax.numpy as jnp
import numpy as np
from jax import lax
from jax.experimental import pallas as pl
from jax.experimental.pallas import tpu as pltpu

F32 = jnp.float32
BF16 = jnp.bfloat16

D_MODEL = 1024
DIFF_HEADS = 8
DIFF_HEAD_DIM = 64
DIFF_V_DIM = 2 * DIFF_HEAD_DIM
ROT_DIM = DIFF_HEAD_DIM // 4
ROPE_THETA = 500000.0
RET_HEADS = 4
RET_KEY_DIM = 256
RET_VALUE_DIM = 512
RET_THETA = 10000.0
RET_CHUNK = 128
FFN_HIDDEN = 2816
NORM_EPS = 1e-5
NEG_INF = -1e30

DIFF_QK_W = DIFF_HEADS * 2 * DIFF_HEAD_DIM
DIFF_V_W = DIFF_HEADS * DIFF_V_DIM
RET_QK_W = RET_HEADS * RET_KEY_DIM
RET_V_W = RET_HEADS * RET_VALUE_DIM
IN_W = 3 * DIFF_QK_W + 2 * RET_QK_W + 2 * RET_V_W + 2 * D_MODEL

LANES = 128
ROPE_TAB_W = 3 * LANES
RET_TAB_W = 3 * RET_KEY_DIM
TAB_W = ROPE_TAB_W + RET_TAB_W
VMEM_LIMIT = 56 * 1024 * 1024
BF16_SUBLANES = 16
ACC_ROWS = DIFF_V_DIM + BF16_SUBLANES
LOG2E = math.log2(math.e)


def _lam_init(layer):
    return 0.8 - 0.6 * math.exp(-0.3 * layer)


def _sigmoid(x):
    return 1.0 / (1.0 + jnp.exp(-x))


def _rms(x):
    return x * lax.rsqrt(jnp.mean(x * x, axis=-1, keepdims=True) + NORM_EPS)


def _resident(shape):
    return pl.BlockSpec(shape, lambda *_: (0,) * len(shape), pipeline_mode=pl.Buffered(1))


def _params(*semantics):
    return pltpu.CompilerParams(dimension_semantics=semantics, vmem_limit_bytes=VMEM_LIMIT)


def _rotation_tables(pos):
    half = ROT_DIM // 2
    freqs = jnp.exp(-math.log(ROPE_THETA) * jnp.arange(half, dtype=F32) * (2.0 / ROT_DIM))
    ang = pos[:, None] * freqs[None, :]
    cos, sin = jnp.cos(ang), jnp.sin(ang)
    rows = pos.shape[0]
    fill = lambda value, width: jnp.full((rows, width), value, F32)
    per_head = LANES // DIFF_HEAD_DIM
    rest = DIFF_HEAD_DIM - ROT_DIM
    rope_c = jnp.tile(jnp.concatenate([cos, cos, fill(1.0, rest)], axis=1), (1, per_head))
    rope_s1 = jnp.tile(jnp.concatenate([-sin, fill(0.0, half + rest)], axis=1), (1, per_head))
    rope_s2 = jnp.tile(jnp.concatenate([fill(0.0, half), sin, fill(0.0, rest)], axis=1), (1, per_head))

    rhalf = RET_KEY_DIM // 2
    angle = jnp.exp(-math.log(RET_THETA) * jnp.linspace(0.0, 1.0, rhalf, dtype=F32))
    rang = pos[:, None] * angle[None, :]
    rcos, rsin = jnp.cos(rang), jnp.sin(rang)
    zero = jnp.zeros_like(rsin)
    interleave = lambda even, odd: jnp.stack([even, odd], axis=-1).reshape(rows, RET_KEY_DIM)
    ret_c = interleave(rcos, rcos)
    ret_s1 = interleave(-rsin, zero)
    ret_s2 = interleave(zero, rsin)
    return jnp.concatenate([rope_c, rope_s1, rope_s2, ret_c, ret_s1, ret_s2], axis=1).astype(F32)


def _partial_rope(acc, tab):
    c, s1, s2 = tab[:, 0:LANES], tab[:, LANES:2 * LANES], tab[:, 2 * LANES:3 * LANES]
    half = ROT_DIM // 2
    outs = []
    for g in range(acc.shape[1] // LANES):
        x = acc[:, g * LANES:(g + 1) * LANES]
        outs.append(x * c + pltpu.roll(x, LANES - half, 1) * s1 + pltpu.roll(x, half, 1) * s2)
    return jnp.concatenate(outs, axis=1)


def _ret_rotate(acc, tab):
    per_head = RET_KEY_DIM // LANES
    outs = []
    for g in range(acc.shape[1] // LANES):
        o = (g % per_head) * LANES
        c = tab[:, o:o + LANES]
        s1 = tab[:, RET_KEY_DIM + o:RET_KEY_DIM + o + LANES]
        s2 = tab[:, 2 * RET_KEY_DIM + o:2 * RET_KEY_DIM + o + LANES]
        x = acc[:, g * LANES:(g + 1) * LANES]
        outs.append(x * c + pltpu.roll(x, LANES - 1, 1) * s1 + pltpu.roll(x, 1, 1) * s2)
    return jnp.concatenate(outs, axis=1)


class _Projector:
    def __init__(self, x_ref, g_ref, w_ref, tab_ref, xn_ref):
        xn_ref[...] = (_rms(x_ref[...]) * g_ref[...]).astype(BF16)
        self._xn_ref, self._w_ref = xn_ref, w_ref
        self._rope_tab = tab_ref[:, 0:ROPE_TAB_W]
        self._ret_tab = tab_ref[:, ROPE_TAB_W:TAB_W]
        self._col = 0

    def _mm(self, n):
        c0, self._col = self._col, self._col + n
        return jnp.dot(self._xn_ref[...], self._w_ref[:, c0:c0 + n], preferred_element_type=F32)

    def qa(self):
        return _partial_rope(self._mm(DIFF_QK_W), self._rope_tab) * (DIFF_HEAD_DIM ** -0.5)

    def ka(self):
        return _partial_rope(self._mm(DIFF_QK_W), self._rope_tab)

    def va(self):
        return self._mm(DIFF_V_W)

    def qr(self):
        return _ret_rotate(self._mm(RET_QK_W), self._ret_tab)

    def kr(self):
        return _ret_rotate(self._mm(RET_QK_W), self._ret_tab) * (RET_KEY_DIM ** -0.5)

    def wide(self):
        return self._mm(RET_V_W)

    def gate(self):
        return self._mm(D_MODEL)


def _proj_sample_kernel(x_ref, g_ref, w_ref, tab_ref, qa_ref, ka_ref, va_ref, qr_ref, kr_ref, vr_ref, gr_ref, ga_ref,
                        gb_ref, xn_ref):
    p = _Projector(x_ref, g_ref, w_ref, tab_ref, xn_ref)
    qa_ref[...] = p.qa()
    ka_ref[...] = p.ka()
    va_ref[...] = p.va()
    qr_ref[...] = p.qr()
    kr_ref[...] = p.kr()
    vr_ref[...] = p.wide()
    gr_ref[...] = p.wide()
    ga_ref[...] = p.gate()
    gb_ref[...] = p.gate()


def _proj_prompt_kernel(x_ref, g_ref, w_ref, tab_ref, qx_ref, kn_ref, kt_ref, vt_ref, vo_ref, qr_ref, kr_ref, vr_ref,
                        gr_ref, ga_ref, gb_ref, xn_ref):
    p = _Projector(x_ref, g_ref, w_ref, tab_ref, xn_ref)
    tm = x_ref.shape[0]
    qt = (p.qa() * LOG2E).T.astype(BF16)
    zeros = jnp.zeros((DIFF_HEAD_DIM, tm), BF16)
    for hc in range(2 * DIFF_HEADS):
        comp = hc % 2
        src = qt[hc * DIFF_HEAD_DIM:(hc + 1) * DIFF_HEAD_DIM]
        lo, hi = (src, zeros) if comp == 0 else (zeros, src)
        qx_ref[hc * DIFF_V_DIM:hc * DIFF_V_DIM + DIFF_HEAD_DIM, :] = lo
        qx_ref[hc * DIFF_V_DIM + DIFF_HEAD_DIM:(hc + 1) * DIFF_V_DIM, :] = hi
    k = p.ka()
    kn_ref[...] = k.astype(BF16)
    kt_ref[...] = k.T
    v = p.va()
    vt_ref[...] = v.T.astype(BF16)
    for h in range(DIFF_HEADS):
        vo_ref[pl.ds(h, tm, stride=DIFF_HEADS), :] = v[:, h * DIFF_V_DIM:(h + 1) * DIFF_V_DIM]
    qr_ref[...] = p.qr().astype(BF16)
    kr_ref[...] = p.kr().astype(BF16)
    vr_ref[...] = p.wide().astype(BF16)
    gr_ref[...] = p.wide()
    ga_ref[...] = p.gate()
    gb_ref[...] = p.gate()


def _proj_in_specs(tm, tab_blocks):
    return [
        pl.BlockSpec((tm, D_MODEL), lambda i: (i, 0)),
        _resident((1, D_MODEL)),
        _resident((D_MODEL, IN_W)),
        pl.BlockSpec((tm, TAB_W), lambda i: (i % tab_blocks, 0)),
    ]


def _project_sample(x2d, gain, w_bf16, tab, tm):
    rows = x2d.shape[0]
    widths = (DIFF_QK_W, DIFF_QK_W, DIFF_V_W, RET_QK_W, RET_QK_W, RET_V_W, RET_V_W, D_MODEL, D_MODEL)
    return pl.pallas_call(
        _proj_sample_kernel,
        grid=(rows // tm,),
        in_specs=_proj_in_specs(tm, tab.shape[0] // tm),
        out_specs=[pl.BlockSpec((tm, w), lambda i: (i, 0)) for w in widths],
        out_shape=[jax.ShapeDtypeStruct((rows, w), F32) for w in widths],
        scratch_shapes=[pltpu.VMEM((tm, D_MODEL), BF16)],
        compiler_params=_params("parallel"),
        name="proj_sample",
    )(x2d, gain.reshape(1, D_MODEL), w_bf16, tab)


def _project_prompt(x2d, gain, w_bf16, tab, batch, seq, tm):
    rows = batch * seq
    per_b = seq // tm
    row = lambda w: pl.BlockSpec((tm, w), lambda i: (i, 0))
    transposed = lambda h: pl.BlockSpec((None, h, tm), lambda i: (i // per_b, 0, i % per_b))
    specs = [
        (transposed(2 * DIFF_QK_W), (batch, 2 * DIFF_QK_W, seq), BF16),
        (row(DIFF_QK_W), (rows, DIFF_QK_W), BF16),
        (transposed(DIFF_QK_W), (batch, DIFF_QK_W, seq), F32),
        (transposed(DIFF_V_W), (batch, DIFF_V_W, seq), BF16),
        (pl.BlockSpec((tm * DIFF_HEADS, DIFF_V_DIM), lambda i: (i, 0)), (rows * DIFF_HEADS, DIFF_V_DIM), F32),
        (row(RET_QK_W), (rows, RET_QK_W), BF16),
        (row(RET_QK_W), (rows, RET_QK_W), BF16),
        (row(RET_V_W), (rows, RET_V_W), BF16),
        (row(RET_V_W), (rows, RET_V_W), F32),
        (row(D_MODEL), (rows, D_MODEL), F32),
        (row(D_MODEL), (rows, D_MODEL), F32),
    ]
    return pl.pallas_call(
        _proj_prompt_kernel,
        grid=(rows // tm,),
        in_specs=_proj_in_specs(tm, tab.shape[0] // tm),
        out_specs=[s for s, _, _ in specs],
        out_shape=[jax.ShapeDtypeStruct(shape, dtype) for _, shape, dtype in specs],
        scratch_shapes=[pltpu.VMEM((tm, D_MODEL), BF16)],
        compiler_params=_params("parallel"),
        name="proj_prompt",
    )(x2d, gain.reshape(1, D_MODEL), w_bf16, tab)


def _lambda(lam_ref, lam_init):
    lv = lam_ref[...]
    s1 = jnp.sum(lv[0:1] * lv[1:2], axis=-1, keepdims=True)
    s2 = jnp.sum(lv[2:3] * lv[3:4], axis=-1, keepdims=True)
    return jnp.exp(s1) - jnp.exp(s2) + lam_init


def _dot_nt(a, b):
    return lax.dot_general(a, b, (((1,), (1,)), ((), ())), preferred_element_type=F32)


def _sub_layer_norm(o, sub_ref, lam_init):
    return _rms(o) * sub_ref[...] * (1.0 - lam_init)


def _attn_kernel(pt_ref, qi_ref, ki_ref, seq_ref, dec_ref, lam_ref, sub_ref, qx_ref, k_ref, vt_ref, qs_ref, kn_ref,
                 vn_ref, *rest, n_pages, lam_init):
    del pt_ref, seq_ref
    kt_pages = rest[:n_pages]
    v_pages = rest[n_pages:2 * n_pages]
    o_ref, os_ref, m_ref, acc_ref, st_ref, mx_ref = rest[2 * n_pages:]
    qi = qi_ref[pl.program_id(1)]
    ki = ki_ref[pl.program_id(1)]
    has_decode = dec_ref[pl.program_id(0) * pl.num_programs(1) + pl.program_id(1)] == 1
    n_stat = 2 * DIFF_HEADS
    tk = k_ref.shape[0]
    lam = _lambda(lam_ref, lam_init)

    @pl.when(ki == 0)
    def _():
        m_ref[...] = jnp.full(m_ref.shape, NEG_INF, F32)
        acc_ref[...] = jnp.zeros(acc_ref.shape, F32)

    def scores(i, on_diagonal):
        h = i // 2
        k_h = k_ref[:, h * DIFF_V_DIM:(h + 1) * DIFF_V_DIM]
        st = jnp.dot(k_h, qx_ref[i * DIFF_V_DIM:(i + 1) * DIFF_V_DIM, :], preferred_element_type=F32)
        if on_diagonal:
            key = lax.broadcasted_iota(jnp.int32, st.shape, 0)
            qry = lax.broadcasted_iota(jnp.int32, st.shape, 1)
            st = jnp.where(key <= qry, st, NEG_INF)
        st_ref[i % 2] = st
        mx_ref[i % 2] = jnp.max(st, axis=0, keepdims=True)

    def step(on_diagonal, with_decode):
        ones = jnp.ones((ACC_ROWS - DIFF_V_DIM, tk), BF16)
        probs = []
        scores(0, on_diagonal)
        for i in range(n_stat):
            if i + 1 < n_stat:
                scores(i + 1, on_diagonal)
            h = i // 2
            vt_h = jnp.concatenate([vt_ref[h * DIFF_V_DIM:(h + 1) * DIFF_V_DIM, :], ones], axis=0)
            m_prev = m_ref[i:i + 1, :]
            m_new = jnp.maximum(m_prev, mx_ref[i % 2])
            p = jnp.exp2(st_ref[i % 2] - m_new).astype(BF16)
            acc_ref[i] = jnp.exp2(m_prev - m_new) * acc_ref[i] + jnp.dot(vt_h, p, preferred_element_type=F32)
            m_ref[i:i + 1, :] = m_new
            if with_decode and i % 2 == 1:
                probs.append(_sample_probs(h, qs_ref, kn_ref, kt_pages))
            if with_decode and i % 4 == 3:
                _sample_values(h - 1, probs[h - 1:h + 1], lam, sub_ref, vn_ref, v_pages, os_ref, lam_init)

    def finish_q_block():
        for h in range(DIFF_HEADS):
            o1, o2 = (acc_ref[i, 0:DIFF_V_DIM, :] / acc_ref[i, DIFF_V_DIM:DIFF_V_DIM + 1, :] for i in (2 * h, 2 * h + 1))
            ot = o1 - lam * o2
            o_ref[:, h * DIFF_V_DIM:(h + 1) * DIFF_V_DIM] = _sub_layer_norm(ot.T, sub_ref, lam_init).astype(o_ref.dtype)

    @pl.when(ki < qi)
    def _():
        step(False, True)

    @pl.when((ki == qi) & has_decode)
    def _():
        step(True, True)
        finish_q_block()

    @pl.when((ki == qi) & jnp.logical_not(has_decode))
    def _():
        step(True, False)
        finish_q_block()


def _attention(page_table, lams, subln, qx, kn, vt, qa_s, ka_s, va_s, kt_pages, v_pages, batch, seq, dec_seq, lam_init,
               blk=512):
    nb = seq // blk
    n_stat = 2 * DIFF_HEADS
    dec_batch, n_pages = page_table.shape
    page = kt_pages.shape[2]
    tiles = [(qi, ki) for qi in range(nb) for ki in range(qi + 1)]
    n_tiles = len(tiles)
    flat = [tile for _ in range(batch) for tile in tiles]
    surplus = len(flat) - dec_batch
    assert 0 <= surplus <= batch * nb, "need at least one tile per decode sequence, surplus only on diagonal tiles"
    decode = [True] * len(flat)
    for f in reversed(range(len(flat))):
        if surplus and flat[f][0] == flat[f][1]:
            decode[f] = False
            surplus -= 1
    seq_tab = list(np.maximum(np.cumsum(decode) - 1, 0))
    qi_of = jnp.asarray([qi for qi, _ in tiles], jnp.int32)
    ki_of = jnp.asarray([ki for _, ki in tiles], jnp.int32)
    seq_of = jnp.asarray(seq_tab, jnp.int32)
    dec_of = jnp.asarray(decode, jnp.int32)

    tok_map = lambda b, t, pt, qo, ko, so, do: (so[b * n_tiles + t], 0)

    def page_spec(j):
        return pl.BlockSpec((None, DIFF_QK_W, page),
                            lambda b, t, pt, qo, ko, so, do: (pt[so[b * n_tiles + t] * n_pages + j], 0, 0))

    const = lambda shape: pl.BlockSpec(shape, lambda b, t, *_: (0,) * len(shape))
    return pl.pallas_call(
        functools.partial(_attn_kernel, n_pages=n_pages, lam_init=lam_init),
        grid_spec=pltpu.PrefetchScalarGridSpec(
            num_scalar_prefetch=5,
            grid=(batch, n_tiles),
            in_specs=[
                const((4, DIFF_HEAD_DIM)),
                const((1, DIFF_V_DIM)),
                pl.BlockSpec((None, 2 * DIFF_QK_W, blk), lambda b, t, pt, qo, ko, so, do: (b, 0, qo[t])),
                pl.BlockSpec((blk, DIFF_QK_W), lambda b, t, pt, qo, ko, so, do: (b * nb + ko[t], 0)),
                pl.BlockSpec((None, DIFF_V_W, blk), lambda b, t, pt, qo, ko, so, do: (b, 0, ko[t])),
                pl.BlockSpec((dec_seq, DIFF_QK_W), tok_map),
                pl.BlockSpec((dec_seq, DIFF_QK_W), tok_map),
                pl.BlockSpec((dec_seq, DIFF_V_W), tok_map),
            ] + [page_spec(j) for j in range(n_pages)] * 2,
            out_specs=[
                pl.BlockSpec((blk, DIFF_V_W), lambda b, t, pt, qo, ko, so, do: (b * nb + qo[t], 0)),
                pl.BlockSpec((dec_seq, DIFF_V_W), tok_map),
            ],
            scratch_shapes=[
                pltpu.VMEM((n_stat, blk), F32),
                pltpu.VMEM((n_stat, ACC_ROWS, blk), F32),
                pltpu.VMEM((2, blk, blk), F32),
                pltpu.VMEM((2, 1, blk), F32),
            ],
        ),
        out_shape=[
            jax.ShapeDtypeStruct((batch * seq, DIFF_V_W), BF16),
            jax.ShapeDtypeStruct((dec_batch * dec_seq, DIFF_V_W), F32),
        ],
        compiler_params=_params("arbitrary", "arbitrary"),
        name="attention",
    )(page_table.reshape(-1), qi_of, ki_of, seq_of, dec_of, lams, subln, qx, kn, vt, qa_s, ka_s, va_s,
      *([kt_pages] * n_pages), *([v_pages] * n_pages))


def _sample_probs(h, q_ref, kn_ref, kt_pages):
    t = q_ref.shape[0]
    page = kt_pages[0].shape[1]
    hs = slice(h * DIFF_V_DIM, (h + 1) * DIFF_V_DIM)
    pad = jnp.zeros((page - t, DIFF_V_DIM), BF16)
    row = lax.broadcasted_iota(jnp.int32, (2 * t, page), 0) % t
    col = lax.broadcasted_iota(jnp.int32, (2 * t, page), 1)
    lane = lax.broadcasted_iota(jnp.int32, (t, DIFF_V_DIM), 1)
    qh = q_ref[:, hs].astype(BF16)
    zero = jnp.zeros_like(qh)
    q = jnp.concatenate([jnp.where(lane < DIFF_HEAD_DIM, qh, zero), jnp.where(lane >= DIFF_HEAD_DIM, qh, zero)], axis=0)
    kt = jnp.concatenate([kp[hs, :] for kp in kt_pages], axis=1).astype(BF16)
    s_cache = jnp.dot(q, kt, preferred_element_type=F32)
    k_new = jnp.concatenate([kn_ref[:, hs].astype(BF16), pad], axis=0)
    s_new = jnp.where(col <= row, _dot_nt(q, k_new), NEG_INF)
    s = jnp.concatenate([s_cache, s_new], axis=1)
    p = jnp.exp(s - jnp.max(s, axis=1, keepdims=True))
    return p.astype(BF16), jnp.sum(p, axis=1, keepdims=True)


def _sample_values(h0, probs, lam, sub_ref, vn_ref, v_pages, o_ref, lam_init):
    (p0, l0), (p1, l1) = probs
    t = vn_ref.shape[0]
    page = v_pages[0].shape[0] // DIFF_HEADS
    p = jnp.concatenate([p0, p1], axis=0)
    pad = jnp.zeros((page - t, 2 * DIFF_V_DIM), BF16)
    v_new = jnp.concatenate([vn_ref[:, h0 * DIFF_V_DIM:(h0 + 2) * DIFF_V_DIM].astype(BF16), pad], axis=0)
    v_cache = [
        jnp.concatenate([vp[pl.ds(h0 + d, page, stride=DIFF_HEADS), :] for d in (0, 1)], axis=1).astype(BF16)
        for vp in v_pages]
    acc = jnp.dot(p, jnp.concatenate(v_cache + [v_new], axis=0), preferred_element_type=F32)
    for d, l in ((0, l0), (1, l1)):
        o = acc[2 * t * d:2 * t * (d + 1), d * DIFF_V_DIM:(d + 1) * DIFF_V_DIM] / l
        hs = slice((h0 + d) * DIFF_V_DIM, (h0 + d + 1) * DIFF_V_DIM)
        o_ref[:, hs] = _sub_layer_norm(o[:t] - lam * o[t:], sub_ref, lam_init).astype(o_ref.dtype)


def _iota_f32(shape, axis):
    return lax.broadcasted_iota(jnp.int32, shape, axis).astype(F32)


def _ret_decays(c, dk, dv, lg):
    rel = _iota_f32((c, c), 0) - _iota_f32((c, c), 1)
    decay = jnp.where(rel >= 0, jnp.exp(lg * jnp.maximum(rel, 0.0)), 0.0)
    q_dec = jnp.exp((_iota_f32((c, dk), 0) + 1.0) * lg)
    k_dec = jnp.exp((c - 1.0 - _iota_f32((c, dk), 0)) * lg)
    s_dec = jnp.exp(jnp.full((1, dv), float(c), F32) * lg)
    return decay, q_dec, k_dec, s_dec


def _ret_chunk(s_prev, q, k, v, decays):
    decay, q_dec, k_dec, s_dec = decays
    qk = _dot_nt(q.astype(BF16), k.astype(BF16)) * decay
    o = jnp.dot(qk.astype(BF16), v, preferred_element_type=F32)
    o = o + jnp.dot((q * q_dec).astype(BF16), s_prev.astype(BF16), preferred_element_type=F32)
    kv = jnp.dot((k * k_dec).T.astype(BF16), v, preferred_element_type=F32)
    return s_dec * s_prev + kv, o


def _gated_ret_out(o, gr):
    return _rms(o) * (gr * _sigmoid(gr))


def _retention_kernel(lg_ref, q_ref, k_ref, v_ref, gr_ref, ss_ref, qs_ref, ks_ref, vs_ref, grs_ref, r_ref, state_ref,
                      rs_ref, sso_ref, s_ref):
    c = pl.program_id(1)

    @pl.when(c == 0)
    def _():
        s_ref[...] = jnp.zeros(s_ref.shape, F32)

    n_stages = 4

    def decode_heads(stage):
        t = qs_ref.shape[0]
        for h in range(stage, RET_HEADS, n_stages):
            ks = slice(h * RET_KEY_DIM, (h + 1) * RET_KEY_DIM)
            vs = slice(h * RET_VALUE_DIM, (h + 1) * RET_VALUE_DIM)
            decays = _ret_decays(t, RET_KEY_DIM, RET_VALUE_DIM, lg_ref[h])
            s_new, o = _ret_chunk(ss_ref[h], qs_ref[:, ks], ks_ref[:, ks], vs_ref[:, vs].astype(BF16), decays)
            sso_ref[h] = s_new
            rs_ref[:, vs] = _gated_ret_out(o, grs_ref[:, vs]).astype(rs_ref.dtype)

    decay, q_dec, k_dec, s_dec = _ret_decays(RET_CHUNK, RET_KEY_DIM, RET_VALUE_DIM, lg_ref[pl.program_id(0)])
    batch = range(q_ref.shape[0])
    qk = [(_dot_nt(q_ref[b], k_ref[b]) * decay).astype(BF16) for b in batch]
    o = [jnp.dot(qk[b], v_ref[b], preferred_element_type=F32) for b in batch]
    decode_heads(0)
    for b in batch:
        qd = (q_ref[b].astype(F32) * q_dec).astype(BF16)
        o[b] = o[b] + jnp.dot(qd, s_ref[b].astype(BF16), preferred_element_type=F32)
    decode_heads(1)
    for b in batch:
        kd = (k_ref[b].astype(F32) * k_dec).T.astype(BF16)
        s_ref[b] = s_dec * s_ref[b] + jnp.dot(kd, v_ref[b], preferred_element_type=F32)
    decode_heads(2)
    for b in batch:
        r_ref[b] = _gated_ret_out(o[b], gr_ref[b]).astype(r_ref.dtype)
    decode_heads(3)

    @pl.when(c == pl.num_programs(1) - 1)
    def _():
        state_ref[...] = s_ref[...]


def _retention(log_gamma, qr, kr, vr, gr, state_s, qr_s, kr_s, vr_s, gr_s, batch, seq, dec_seq):
    nc = seq // RET_CHUNK
    dec_batch = state_s.shape[0]
    assert RET_HEADS * nc >= dec_batch
    seq_of = lambda h, c: jnp.minimum(h * nc + c, dec_batch - 1)
    tok = lambda w: pl.BlockSpec((batch, RET_CHUNK, w), lambda h, c: (0, c, h))
    tok_s = lambda w: pl.BlockSpec((dec_seq, w), lambda h, c: (seq_of(h, c), 0))
    state_s_spec = pl.BlockSpec((None, RET_HEADS, RET_KEY_DIM, RET_VALUE_DIM), lambda h, c: (seq_of(h, c), 0, 0, 0))
    return pl.pallas_call(
        _retention_kernel,
        grid=(RET_HEADS, nc),
        in_specs=[pl.BlockSpec(memory_space=pltpu.SMEM), tok(RET_KEY_DIM), tok(RET_KEY_DIM), tok(RET_VALUE_DIM),
                  tok(RET_VALUE_DIM), state_s_spec, tok_s(RET_QK_W), tok_s(RET_QK_W), tok_s(RET_V_W), tok_s(RET_V_W)],
        out_specs=[
            tok(RET_VALUE_DIM),
            pl.BlockSpec((batch, None, RET_KEY_DIM, RET_VALUE_DIM), lambda h, c: (0, h, 0, 0)),
            tok_s(RET_V_W),
            state_s_spec,
        ],
        out_shape=[
            jax.ShapeDtypeStruct((batch, seq, RET_V_W), BF16),
            jax.ShapeDtypeStruct((batch, RET_HEADS, RET_KEY_DIM, RET_VALUE_DIM), F32),
            jax.ShapeDtypeStruct((dec_batch * dec_seq, RET_V_W), F32),
            jax.ShapeDtypeStruct(state_s.shape, F32),
        ],
        scratch_shapes=[pltpu.VMEM((batch, RET_KEY_DIM, RET_VALUE_DIM), F32)],
        compiler_params=_params("arbitrary", "arbitrary"),
        name="retention",
    )(log_gamma, qr, kr, vr, gr, state_s, qr_s, kr_s, vr_s, gr_s)


def _merge_ffn_kernel(h_ref, a_ref, r_ref, ga_ref, gb_ref, bg_ref, wpa_ref, wpb_ref, wo_ref, gffn_ref, wgu_ref,
                      wdn_ref, gfin_ref, y_ref):
    pa = jnp.dot(a_ref[...].astype(BF16), wpa_ref[...], preferred_element_type=F32)
    pb = jnp.dot(r_ref[...].astype(BF16), wpb_ref[...], preferred_element_type=F32)
    m = (_sigmoid(ga_ref[...] + bg_ref[:, :D_MODEL]) * pa + _sigmoid(gb_ref[...] + bg_ref[:, D_MODEL:]) * pb)
    h1 = h_ref[...] + jnp.dot(m.astype(BF16), wo_ref[...], preferred_element_type=F32)
    xn = (_rms(h1) * gffn_ref[...]).astype(BF16)
    gu = jnp.dot(xn, wgu_ref[...], preferred_element_type=F32)
    g, u = gu[:, :FFN_HIDDEN], gu[:, FFN_HIDDEN:]
    hid = (g * _sigmoid(g) * u).astype(BF16)
    h2 = h1 + jnp.dot(hid, wdn_ref[...], preferred_element_type=F32)
    y_ref[...] = _rms(h2) * gfin_ref[...]


def _merge_ffn(h2d, a, r, ga, gb, b_gate, w_pa, w_pb, w_o, g_ffn, w_gu, w_dn, g_fin, tm):
    rows = h2d.shape[0]
    row = lambda w: pl.BlockSpec((tm, w), lambda i: (i, 0))
    return pl.pallas_call(
        _merge_ffn_kernel,
        grid=(rows // tm,),
        in_specs=[
            row(D_MODEL), row(DIFF_V_W), row(RET_V_W), row(D_MODEL), row(D_MODEL),
            _resident((1, 2 * D_MODEL)),
            _resident((DIFF_V_W, D_MODEL)), _resident((RET_V_W, D_MODEL)), _resident((D_MODEL, D_MODEL)),
            _resident((1, D_MODEL)),
            _resident((D_MODEL, 2 * FFN_HIDDEN)), _resident((FFN_HIDDEN, D_MODEL)),
            _resident((1, D_MODEL)),
        ],
        out_specs=row(D_MODEL),
        out_shape=jax.ShapeDtypeStruct((rows, D_MODEL), F32),
        compiler_params=_params("parallel"),
        name="merge_ffn",
    )(h2d, a, r, ga, gb, b_gate.reshape(1, -1), w_pa, w_pb, w_o, g_ffn.reshape(1, -1), w_gu, w_dn,
      g_fin.reshape(1, -1))


def kernel(x_prompt, x_sample, cache_k, cache_v, state_ret, page_table, norm_mix_g, w_in, b_gate, lambda_q1, lambda_k1,
           lambda_q2, lambda_k2, subln_g, w_pa, w_pb, w_o, norm_ffn_g, w_gu, w_down, norm_final_g):
    batch, seq, _ = x_prompt.shape
    dec_batch, dec_seq, _ = x_sample.shape
    depth, n_phys, page = cache_k.shape[:3]
    assert depth == 1, "the final norm is fused into the (single) layer's FFN kernel"
    past = page_table.shape[1] * page
    tm = 256

    log_gamma = jnp.log(1.0 - jnp.exp2(-5.0 - jnp.arange(RET_HEADS, dtype=F32)))
    tab_p = _rotation_tables(jnp.arange(seq, dtype=F32))
    tab_s = jnp.tile(_rotation_tables(jnp.arange(dec_seq, dtype=F32) + past), (tm // dec_seq, 1))

    l = 0
    lam_init = _lam_init(l)
    lams = jnp.stack([lambda_q1[l], lambda_k1[l], lambda_q2[l], lambda_k2[l]]).astype(F32)
    subln = subln_g[l].reshape(1, DIFF_V_DIM).astype(F32)
    w_in_b = w_in[l].astype(BF16)
    weights = (b_gate[l], w_pa[l].astype(BF16), w_pb[l].astype(BF16), w_o[l].astype(BF16), norm_ffn_g[l],
               w_gu[l].astype(BF16), w_down[l].astype(BF16), norm_final_g)
    kt_pages = jnp.transpose(cache_k[l], (0, 2, 3, 4, 1)).reshape(n_phys, DIFF_QK_W, page)
    v_pages = cache_v[l].reshape(n_phys, page * DIFF_HEADS, DIFF_V_DIM)

    hp = x_prompt.reshape(batch * seq, D_MODEL)
    hs = x_sample.reshape(dec_batch * dec_seq, D_MODEL)
    qx, kn, kt_p, vt, v_p, qr_p, kr_p, vr_p, gr_p, ga_p, gb_p = _project_prompt(
        hp, norm_mix_g[l], w_in_b, tab_p, batch, seq, tm)
    qa_s, ka_s, va_s, qr_s, kr_s, vr_s, gr_s, ga_s, gb_s = _project_sample(hs, norm_mix_g[l], w_in_b, tab_s, tm)
    a_p, a_s = _attention(page_table, lams, subln, qx, kn, vt, qa_s, ka_s, va_s, kt_pages, v_pages, batch, seq,
                          dec_seq, lam_init)

    as_seq = lambda x: x.reshape(batch, seq, x.shape[-1])
    r_p, s_p, r_s, s_s = _retention(log_gamma, as_seq(qr_p), as_seq(kr_p), as_seq(vr_p), as_seq(gr_p), state_ret[l],
                                    qr_s, kr_s, vr_s, gr_s, batch, seq, dec_seq)
    y_p = _merge_ffn(hp, a_p, r_p.reshape(batch * seq, RET_V_W), ga_p, gb_p, *weights, tm)
    y_s = _merge_ffn(hs, a_s, r_s, ga_s, gb_s, *weights, tm)

    k_p = jnp.transpose(kt_p.reshape(1, batch, DIFF_HEADS, 2, DIFF_HEAD_DIM, seq), (0, 1, 5, 2, 3, 4))
    return (
        y_p.reshape(batch, seq, D_MODEL),
        y_s.reshape(dec_batch, dec_seq, D_MODEL),
        k_p,
        v_p.reshape(1, batch, seq, DIFF_HEADS, DIFF_V_DIM),
        s_p[None],
        ka_s.reshape(1, dec_batch, dec_seq, DIFF_HEADS, 2, DIFF_HEAD_DIM),
        va_s.reshape(1, dec_batch, dec_seq, DIFF_HEADS, DIFF_V_DIM),
        s_s[None],
    )
```

```python
import functools
import math

import jax
import jax.numpy as jnp
import numpy as np
from jax import lax
from jax.experimental import pallas as pl
from jax.experimental.pallas import tpu as pltpu

F32 = jnp.float32
BF16 = jnp.bfloat16

D_MODEL = 1024
DIFF_HEADS = 8
DIFF_HEAD_DIM = 64
DIFF_V_DIM = 2 * DIFF_HEAD_DIM
ROT_DIM = DIFF_HEAD_DIM // 4
ROPE_THETA = 500000.0
RET_HEADS = 4
RET_KEY_DIM = 256
RET_VALUE_DIM = 512
RET_THETA = 10000.0
RET_CHUNK = 128
FFN_HIDDEN = 2816
NORM_EPS = 1e-5
NEG_INF = -1e30

DIFF_QK_W = DIFF_HEADS * 2 * DIFF_HEAD_DIM
DIFF_V_W = DIFF_HEADS * DIFF_V_DIM
RET_QK_W = RET_HEADS * RET_KEY_DIM
RET_V_W = RET_HEADS * RET_VALUE_DIM
IN_W = 3 * DIFF_QK_W + 2 * RET_QK_W + 2 * RET_V_W + 2 * D_MODEL

LANES = 128
ROPE_TAB_W = 3 * LANES
RET_TAB_W = 3 * RET_KEY_DIM
TAB_W = ROPE_TAB_W + RET_TAB_W
VMEM_LIMIT = 56 * 1024 * 1024
BF16_SUBLANES = 16
ACC_ROWS = DIFF_V_DIM + BF16_SUBLANES
LOG2E = math.log2(math.e)


def _lam_init(layer):
    return 0.8 - 0.6 * math.exp(-0.3 * layer)


def _sigmoid(x):
    return 1.0 / (1.0 + jnp.exp(-x))


def _rms(x):
    return x * lax.rsqrt(jnp.mean(x * x, axis=-1, keepdims=True) + NORM_EPS)


def _resident(shape):
    return pl.BlockSpec(shape, lambda *_: (0,) * len(shape), pipeline_mode=pl.Buffered(1))


def _params(*semantics):
    return pltpu.CompilerParams(dimension_semantics=semantics, vmem_limit_bytes=VMEM_LIMIT)


def _rotation_tables(pos):
    pos = np.asarray(pos, np.float64)
    half = ROT_DIM // 2
    freqs = np.exp(-math.log(ROPE_THETA) * np.arange(half) * (2.0 / ROT_DIM))
    ang = pos[:, None] * freqs[None, :]
    cos, sin = np.cos(ang), np.sin(ang)
    lane = np.arange(LANES) % DIFF_HEAD_DIM
    idx = lane % half
    rope_c = np.where(lane < ROT_DIM, cos[:, idx], 1.0)
    rope_s1 = np.where(lane < half, -sin[:, idx], 0.0)
    rope_s2 = np.where((lane >= half) & (lane < ROT_DIM), sin[:, idx], 0.0)

    rhalf = RET_KEY_DIM // 2
    angle = np.exp(-math.log(RET_THETA) * np.linspace(0.0, 1.0, rhalf))
    rang = pos[:, None] * angle[None, :]
    rcos, rsin = np.cos(rang), np.sin(rang)
    lane2 = np.arange(RET_KEY_DIM)
    pair = lane2 // 2
    even = lane2 % 2 == 0
    ret_c = rcos[:, pair]
    ret_s1 = np.where(even, -rsin[:, pair], 0.0)
    ret_s2 = np.where(~even, rsin[:, pair], 0.0)
    return np.concatenate([rope_c, rope_s1, rope_s2, ret_c, ret_s1, ret_s2], axis=1).astype(np.float32)


def _partial_rope(acc, tab):
    c, s1, s2 = tab[:, 0:LANES], tab[:, LANES:2 * LANES], tab[:, 2 * LANES:3 * LANES]
    half = ROT_DIM // 2
    outs = []
    for g in range(acc.shape[1] // LANES):
        x = acc[:, g * LANES:(g + 1) * LANES]
        outs.append(x * c + pltpu.roll(x, LANES - half, 1) * s1 + pltpu.roll(x, half, 1) * s2)
    return jnp.concatenate(outs, axis=1)


def _ret_rotate(acc, tab):
    per_head = RET_KEY_DIM // LANES
    outs = []
    for g in range(acc.shape[1] // LANES):
        o = (g % per_head) * LANES
        c = tab[:, o:o + LANES]
        s1 = tab[:, RET_KEY_DIM + o:RET_KEY_DIM + o + LANES]
        s2 = tab[:, 2 * RET_KEY_DIM + o:2 * RET_KEY_DIM + o + LANES]
        x = acc[:, g * LANES:(g + 1) * LANES]
        outs.append(x * c + pltpu.roll(x, LANES - 1, 1) * s1 + pltpu.roll(x, 1, 1) * s2)
    return jnp.concatenate(outs, axis=1)


class _Projector:
    def __init__(self, x_ref, g_ref, w_ref, tab_ref, xn_ref):
        xn_ref[...] = (_rms(x_ref[...]) * g_ref[...]).astype(BF16)
        self._xn_ref, self._w_ref = xn_ref, w_ref
        self._rope_tab = tab_ref[:, 0:ROPE_TAB_W]
        self._ret_tab = tab_ref[:, ROPE_TAB_W:TAB_W]
        self._col = 0

    def _mm(self, n):
        c0, self._col = self._col, self._col + n
        return jnp.dot(self._xn_ref[...], self._w_ref[:, c0:c0 + n], preferred_element_type=F32)

    def qa(self):
        return _partial_rope(self._mm(DIFF_QK_W), self._rope_tab) * (DIFF_HEAD_DIM ** -0.5)

    def ka(self):
        return _partial_rope(self._mm(DIFF_QK_W), self._rope_tab)

    def va(self):
        return self._mm(DIFF_V_W)

    def qr(self):
        return _ret_rotate(self._mm(RET_QK_W), self._ret_tab)

    def kr(self):
        return _ret_rotate(self._mm(RET_QK_W), self._ret_tab) * (RET_KEY_DIM ** -0.5)

    def wide(self):
        return self._mm(RET_V_W)

    def gate(self):
        return self._mm(D_MODEL)


def _proj_sample_kernel(x_ref, g_ref, w_ref, tab_ref, qa_ref, ka_ref, va_ref, qr_ref, kr_ref, vr_ref, gr_ref, ga_ref,
                        gb_ref, xn_ref):
    p = _Projector(x_ref, g_ref, w_ref, tab_ref, xn_ref)
    qa_ref[...] = p.qa()
    ka_ref[...] = p.ka()
    va_ref[...] = p.va()
    qr_ref[...] = p.qr()
    kr_ref[...] = p.kr()
    vr_ref[...] = p.wide()
    gr_ref[...] = p.wide()
    ga_ref[...] = p.gate()
    gb_ref[...] = p.gate()


def _proj_prompt_kernel(x_ref, g_ref, w_ref, tab_ref, qx_ref, kn_ref, kt_ref, vt_ref, vo_ref, qr_ref, kr_ref, vr_ref,
                        gr_ref, ga_ref, gb_ref, xn_ref):
    p = _Projector(x_ref, g_ref, w_ref, tab_ref, xn_ref)
    tm = x_ref.shape[0]
    qt = (p.qa() * LOG2E).T.astype(BF16)
    zeros = jnp.zeros((DIFF_HEAD_DIM, tm), BF16)
    for hc in range(2 * DIFF_HEADS):
        comp = hc % 2
        src = qt[hc * DIFF_HEAD_DIM:(hc + 1) * DIFF_HEAD_DIM]
        lo, hi = (src, zeros) if comp == 0 else (zeros, src)
        qx_ref[hc * DIFF_V_DIM:hc * DIFF_V_DIM + DIFF_HEAD_DIM, :] = lo
        qx_ref[hc * DIFF_V_DIM + DIFF_HEAD_DIM:(hc + 1) * DIFF_V_DIM, :] = hi
    k = p.ka()
    kn_ref[...] = k.astype(BF16)
    kt_ref[...] = k.T
    v = p.va()
    vt_ref[...] = v.T.astype(BF16)
    for h in range(DIFF_HEADS):
        vo_ref[pl.ds(h, tm, stride=DIFF_HEADS), :] = v[:, h * DIFF_V_DIM:(h + 1) * DIFF_V_DIM]
    qr_ref[...] = p.qr().astype(BF16)
    kr_ref[...] = p.kr().astype(BF16)
    vr_ref[...] = p.wide().astype(BF16)
    gr_ref[...] = p.wide()
    ga_ref[...] = p.gate()
    gb_ref[...] = p.gate()


def _proj_in_specs(tm, tab_blocks):
    return [
        pl.BlockSpec((tm, D_MODEL), lambda i: (i, 0)),
        _resident((1, D_MODEL)),
        _resident((D_MODEL, IN_W)),
        pl.BlockSpec((tm, TAB_W), lambda i: (i % tab_blocks, 0)),
    ]


def _project_sample(x2d, gain, w_bf16, tab, tm):
    rows = x2d.shape[0]
    widths = (DIFF_QK_W, DIFF_QK_W, DIFF_V_W, RET_QK_W, RET_QK_W, RET_V_W, RET_V_W, D_MODEL, D_MODEL)
    return pl.pallas_call(
        _proj_sample_kernel,
        grid=(rows // tm,),
        in_specs=_proj_in_specs(tm, tab.shape[0] // tm),
        out_specs=[pl.BlockSpec((tm, w), lambda i: (i, 0)) for w in widths],
        out_shape=[jax.ShapeDtypeStruct((rows, w), F32) for w in widths],
        scratch_shapes=[pltpu.VMEM((tm, D_MODEL), BF16)],
        compiler_params=_params("parallel"),
        name="proj_sample",
    )(x2d, gain.reshape(1, D_MODEL), w_bf16, tab)


def _project_prompt(x2d, gain, w_bf16, tab, batch, seq, tm):
    rows = batch * seq
    per_b = seq // tm
    row = lambda w: pl.BlockSpec((tm, w), lambda i: (i, 0))
    transposed = lambda h: pl.BlockSpec((None, h, tm), lambda i: (i // per_b, 0, i % per_b))
    specs = [
        (transposed(2 * DIFF_QK_W), (batch, 2 * DIFF_QK_W, seq), BF16),
        (row(DIFF_QK_W), (rows, DIFF_QK_W), BF16),
        (transposed(DIFF_QK_W), (batch, DIFF_QK_W, seq), F32),
        (transposed(DIFF_V_W), (batch, DIFF_V_W, seq), BF16),
        (pl.BlockSpec((tm * DIFF_HEADS, DIFF_V_DIM), lambda i: (i, 0)), (rows * DIFF_HEADS, DIFF_V_DIM), F32),
        (row(RET_QK_W), (rows, RET_QK_W), BF16),
        (row(RET_QK_W), (rows, RET_QK_W), BF16),
        (row(RET_V_W), (rows, RET_V_W), BF16),
        (row(RET_V_W), (rows, RET_V_W), F32),
        (row(D_MODEL), (rows, D_MODEL), F32),
        (row(D_MODEL), (rows, D_MODEL), F32),
    ]
    return pl.pallas_call(
        _proj_prompt_kernel,
        grid=(rows // tm,),
        in_specs=_proj_in_specs(tm, tab.shape[0] // tm),
        out_specs=[s for s, _, _ in specs],
        out_shape=[jax.ShapeDtypeStruct(shape, dtype) for _, shape, dtype in specs],
        scratch_shapes=[pltpu.VMEM((tm, D_MODEL), BF16)],
        compiler_params=_params("parallel"),
        name="proj_prompt",
    )(x2d, gain.reshape(1, D_MODEL), w_bf16, tab)


def _lambda(lam_ref, lam_init):
    lv = lam_ref[...]
    s1 = jnp.sum(lv[0:1] * lv[1:2], axis=-1, keepdims=True)
    s2 = jnp.sum(lv[2:3] * lv[3:4], axis=-1, keepdims=True)
    return jnp.exp(s1) - jnp.exp(s2) + lam_init


def _dot_nt(a, b):
    return lax.dot_general(a, b, (((1,), (1,)), ((), ())), preferred_element_type=F32)


def _sub_layer_norm(o, sub_ref, lam_init):
    return _rms(o) * sub_ref[...] * (1.0 - lam_init)


def _attn_kernel(pt_ref, qi_ref, ki_ref, seq_ref, dec_ref, lam_ref, sub_ref, qx_ref, k_ref, vt_ref, qs_ref, kn_ref,
                 vn_ref, *rest, n_pages, lam_init):
    del pt_ref, seq_ref
    kt_pages = rest[:n_pages]
    v_pages = rest[n_pages:2 * n_pages]
    o_ref, os_ref, m_ref, acc_ref, st_ref, mx_ref = rest[2 * n_pages:]
    qi = qi_ref[pl.program_id(1)]
    ki = ki_ref[pl.program_id(1)]
    has_decode = dec_ref[pl.program_id(0) * pl.num_programs(1) + pl.program_id(1)] == 1
    n_stat = 2 * DIFF_HEADS
    tk = k_ref.shape[0]
    lam = _lambda(lam_ref, lam_init)

    @pl.when(ki == 0)
    def _():
        m_ref[...] = jnp.full(m_ref.shape, NEG_INF, F32)
        acc_ref[...] = jnp.zeros(acc_ref.shape, F32)

    def scores(i, on_diagonal):
        h = i // 2
        k_h = k_ref[:, h * DIFF_V_DIM:(h + 1) * DIFF_V_DIM]
        st = jnp.dot(k_h, qx_ref[i * DIFF_V_DIM:(i + 1) * DIFF_V_DIM, :], preferred_element_type=F32)
        if on_diagonal:
            key = lax.broadcasted_iota(jnp.int32, st.shape, 0)
            qry = lax.broadcasted_iota(jnp.int32, st.shape, 1)
            st = jnp.where(key <= qry, st, NEG_INF)
        st_ref[i % 2] = st
        mx_ref[i % 2] = jnp.max(st, axis=0, keepdims=True)

    def step(on_diagonal, with_decode):
        ones = jnp.ones((ACC_ROWS - DIFF_V_DIM, tk), BF16)
        probs = []
        scores(0, on_diagonal)
        for i in range(n_stat):
            if i + 1 < n_stat:
                scores(i + 1, on_diagonal)
            h = i // 2
            vt_h = jnp.concatenate([vt_ref[h * DIFF_V_DIM:(h + 1) * DIFF_V_DIM, :], ones], axis=0)
            m_prev = m_ref[i:i + 1, :]
            m_new = jnp.maximum(m_prev, mx_ref[i % 2])
            p = jnp.exp2(st_ref[i % 2] - m_new).astype(BF16)
            acc_ref[i] = jnp.exp2(m_prev - m_new) * acc_ref[i] + jnp.dot(vt_h, p, preferred_element_type=F32)
            m_ref[i:i + 1, :] = m_new
            if with_decode and i % 2 == 1:
                probs.append(_sample_probs(h, qs_ref, kn_ref, kt_pages))
            if with_decode and i % 4 == 3:
                _sample_values(h - 1, probs[h - 1:h + 1], lam, sub_ref, vn_ref, v_pages, os_ref, lam_init)

    def finish_q_block():
        for h in range(DIFF_HEADS):
            o1, o2 = (acc_ref[i, 0:DIFF_V_DIM, :] / acc_ref[i, DIFF_V_DIM:DIFF_V_DIM + 1, :] for i in (2 * h, 2 * h + 1))
            ot = o1 - lam * o2
            o_ref[:, h * DIFF_V_DIM:(h + 1) * DIFF_V_DIM] = _sub_layer_norm(ot.T, sub_ref, lam_init).astype(o_ref.dtype)

    @pl.when(ki < qi)
    def _():
        step(False, True)

    @pl.when((ki == qi) & has_decode)
    def _():
        step(True, True)
        finish_q_block()

    @pl.when((ki == qi) & jnp.logical_not(has_decode))
    def _():
        step(True, False)
        finish_q_block()


def _attention(page_table, lams, subln, qx, kn, vt, qa_s, ka_s, va_s, kt_pages, v_pages, batch, seq, dec_seq, lam_init,
               blk=512):
    nb = seq // blk
    n_stat = 2 * DIFF_HEADS
    dec_batch, n_pages = page_table.shape
    page = kt_pages.shape[2]
    tiles = [(qi, ki) for qi in range(nb) for ki in range(qi + 1)]
    n_tiles = len(tiles)
    flat = [tile for _ in range(batch) for tile in tiles]
    surplus = len(flat) - dec_batch
    assert 0 <= surplus <= batch * nb, "need at least one tile per decode sequence, surplus only on diagonal tiles"
    decode = [True] * len(flat)
    for f in reversed(range(len(flat))):
        if surplus and flat[f][0] == flat[f][1]:
            decode[f] = False
            surplus -= 1
    seq_tab = list(np.maximum(np.cumsum(decode) - 1, 0))
    qi_of = jnp.asarray([qi for qi, _ in tiles], jnp.int32)
    ki_of = jnp.asarray([ki for _, ki in tiles], jnp.int32)
    seq_of = jnp.asarray(seq_tab, jnp.int32)
    dec_of = jnp.asarray(decode, jnp.int32)

    tok_map = lambda b, t, pt, qo, ko, so, do: (so[b * n_tiles + t], 0)

    def page_spec(j):
        return pl.BlockSpec((None, DIFF_QK_W, page),
                            lambda b, t, pt, qo, ko, so, do: (pt[so[b * n_tiles + t] * n_pages + j], 0, 0))

    const = lambda shape: pl.BlockSpec(shape, lambda b, t, *_: (0,) * len(shape))
    return pl.pallas_call(
        functools.partial(_attn_kernel, n_pages=n_pages, lam_init=lam_init),
        grid_spec=pltpu.PrefetchScalarGridSpec(
            num_scalar_prefetch=5,
            grid=(batch, n_tiles),
            in_specs=[
                const((4, DIFF_HEAD_DIM)),
                const((1, DIFF_V_DIM)),
                pl.BlockSpec((None, 2 * DIFF_QK_W, blk), lambda b, t, pt, qo, ko, so, do: (b, 0, qo[t])),
                pl.BlockSpec((blk, DIFF_QK_W), lambda b, t, pt, qo, ko, so, do: (b * nb + ko[t], 0)),
                pl.BlockSpec((None, DIFF_V_W, blk), lambda b, t, pt, qo, ko, so, do: (b, 0, ko[t])),
                pl.BlockSpec((dec_seq, DIFF_QK_W), tok_map),
                pl.BlockSpec((dec_seq, DIFF_QK_W), tok_map),
                pl.BlockSpec((dec_seq, DIFF_V_W), tok_map),
            ] + [page_spec(j) for j in range(n_pages)] * 2,
            out_specs=[
                pl.BlockSpec((blk, DIFF_V_W), lambda b, t, pt, qo, ko, so, do: (b * nb + qo[t], 0)),
                pl.BlockSpec((dec_seq, DIFF_V_W), tok_map),
            ],
            scratch_shapes=[
                pltpu.VMEM((n_stat, blk), F32),
                pltpu.VMEM((n_stat, ACC_ROWS, blk), F32),
                pltpu.VMEM((2, blk, blk), F32),
                pltpu.VMEM((2, 1, blk), F32),
            ],
        ),
        out_shape=[
            jax.ShapeDtypeStruct((batch * seq, DIFF_V_W), BF16),
            jax.ShapeDtypeStruct((dec_batch * dec_seq, DIFF_V_W), F32),
        ],
        compiler_params=_params("arbitrary", "arbitrary"),
        name="attention",
    )(page_table.reshape(-1), qi_of, ki_of, seq_of, dec_of, lams, subln, qx, kn, vt, qa_s, ka_s, va_s,
      *([kt_pages] * n_pages), *([v_pages] * n_pages))


def _sample_probs(h, q_ref, kn_ref, kt_pages):
    t = q_ref.shape[0]
    page = kt_pages[0].shape[1]
    hs = slice(h * DIFF_V_DIM, (h + 1) * DIFF_V_DIM)
    pad = jnp.zeros((page - t, DIFF_V_DIM), BF16)
    row = lax.broadcasted_iota(jnp.int32, (2 * t, page), 0) % t
    col = lax.broadcasted_iota(jnp.int32, (2 * t, page), 1)
    lane = lax.broadcasted_iota(jnp.int32, (t, DIFF_V_DIM), 1)
    qh = q_ref[:, hs].astype(BF16)
    zero = jnp.zeros_like(qh)
    q = jnp.concatenate([jnp.where(lane < DIFF_HEAD_DIM, qh, zero), jnp.where(lane >= DIFF_HEAD_DIM, qh, zero)], axis=0)
    kt = jnp.concatenate([kp[hs, :] for kp in kt_pages], axis=1).astype(BF16)
    s_cache = jnp.dot(q, kt, preferred_element_type=F32)
    k_new = jnp.concatenate([kn_ref[:, hs].astype(BF16), pad], axis=0)
    s_new = jnp.where(col <= row, _dot_nt(q, k_new), NEG_INF)
    s = jnp.concatenate([s_cache, s_new], axis=1)
    p = jnp.exp(s - jnp.max(s, axis=1, keepdims=True))
    return p.astype(BF16), jnp.sum(p, axis=1, keepdims=True)


def _sample_values(h0, probs, lam, sub_ref, vn_ref, v_pages, o_ref, lam_init):
    (p0, l0), (p1, l1) = probs
    t = vn_ref.shape[0]
    page = v_pages[0].shape[0] // DIFF_HEADS
    p = jnp.concatenate([p0, p1], axis=0)
    pad = jnp.zeros((page - t, 2 * DIFF_V_DIM), BF16)
    v_new = jnp.concatenate([vn_ref[:, h0 * DIFF_V_DIM:(h0 + 2) * DIFF_V_DIM].astype(BF16), pad], axis=0)
    v_cache = [
        jnp.concatenate([vp[pl.ds(h0 + d, page, stride=DIFF_HEADS), :] for d in (0, 1)], axis=1).astype(BF16)
        for vp in v_pages]
    acc = jnp.dot(p, jnp.concatenate(v_cache + [v_new], axis=0), preferred_element_type=F32)
    for d, l in ((0, l0), (1, l1)):
        o = acc[2 * t * d:2 * t * (d + 1), d * DIFF_V_DIM:(d + 1) * DIFF_V_DIM] / l
        hs = slice((h0 + d) * DIFF_V_DIM, (h0 + d + 1) * DIFF_V_DIM)
        o_ref[:, hs] = _sub_layer_norm(o[:t] - lam * o[t:], sub_ref, lam_init).astype(o_ref.dtype)


def _iota_f32(shape, axis):
    return lax.broadcasted_iota(jnp.int32, shape, axis).astype(F32)


def _ret_decays(c, dk, dv, lg):
    rel = _iota_f32((c, c), 0) - _iota_f32((c, c), 1)
    decay = jnp.where(rel >= 0, jnp.exp(lg * jnp.maximum(rel, 0.0)), 0.0)
    q_dec = jnp.exp((_iota_f32((c, dk), 0) + 1.0) * lg)
    k_dec = jnp.exp((c - 1.0 - _iota_f32((c, dk), 0)) * lg)
    s_dec = jnp.exp(jnp.full((1, dv), float(c), F32) * lg)
    return decay, q_dec, k_dec, s_dec


def _ret_chunk(s_prev, q, k, v, decays):
    decay, q_dec, k_dec, s_dec = decays
    qk = _dot_nt(q.astype(BF16), k.astype(BF16)) * decay
    o = jnp.dot(qk.astype(BF16), v, preferred_element_type=F32)
    o = o + jnp.dot((q * q_dec).astype(BF16), s_prev.astype(BF16), preferred_element_type=F32)
    kv = jnp.dot((k * k_dec).T.astype(BF16), v, preferred_element_type=F32)
    return s_dec * s_prev + kv, o


def _gated_ret_out(o, gr):
    return _rms(o) * (gr * _sigmoid(gr))


def _retention_kernel(lg_ref, q_ref, k_ref, v_ref, gr_ref, ss_ref, qs_ref, ks_ref, vs_ref, grs_ref, r_ref, state_ref,
                      rs_ref, sso_ref, s_ref):
    c = pl.program_id(1)

    @pl.when(c == 0)
    def _():
        s_ref[...] = jnp.zeros(s_ref.shape, F32)

    n_stages = 4

    def decode_heads(stage):
        t = qs_ref.shape[0]
        for h in range(stage, RET_HEADS, n_stages):
            ks = slice(h * RET_KEY_DIM, (h + 1) * RET_KEY_DIM)
            vs = slice(h * RET_VALUE_DIM, (h + 1) * RET_VALUE_DIM)
            decays = _ret_decays(t, RET_KEY_DIM, RET_VALUE_DIM, lg_ref[h])
            s_new, o = _ret_chunk(ss_ref[h], qs_ref[:, ks], ks_ref[:, ks], vs_ref[:, vs].astype(BF16), decays)
            sso_ref[h] = s_new
            rs_ref[:, vs] = _gated_ret_out(o, grs_ref[:, vs]).astype(rs_ref.dtype)

    decay, q_dec, k_dec, s_dec = _ret_decays(RET_CHUNK, RET_KEY_DIM, RET_VALUE_DIM, lg_ref[pl.program_id(0)])
    batch = range(q_ref.shape[0])
    qk = [(_dot_nt(q_ref[b], k_ref[b]) * decay).astype(BF16) for b in batch]
    o = [jnp.dot(qk[b], v_ref[b], preferred_element_type=F32) for b in batch]
    decode_heads(0)
    for b in batch:
        qd = (q_ref[b].astype(F32) * q_dec).astype(BF16)
        o[b] = o[b] + jnp.dot(qd, s_ref[b].astype(BF16), preferred_element_type=F32)
    decode_heads(1)
    for b in batch:
        kd = (k_ref[b].astype(F32) * k_dec).T.astype(BF16)
        s_ref[b] = s_dec * s_ref[b] + jnp.dot(kd, v_ref[b], preferred_element_type=F32)
    decode_heads(2)
    for b in batch:
        r_ref[b] = _gated_ret_out(o[b], gr_ref[b]).astype(r_ref.dtype)
    decode_heads(3)

    @pl.when(c == pl.num_programs(1) - 1)
    def _():
        state_ref[...] = s_ref[...]


def _retention(log_gamma, qr, kr, vr, gr, state_s, qr_s, kr_s, vr_s, gr_s, batch, seq, dec_seq):
    nc = seq // RET_CHUNK
    dec_batch = state_s.shape[0]
    assert RET_HEADS * nc >= dec_batch
    seq_of = lambda h, c: jnp.minimum(h * nc + c, dec_batch - 1)
    tok = lambda w: pl.BlockSpec((batch, RET_CHUNK, w), lambda h, c: (0, c, h))
    tok_s = lambda w: pl.BlockSpec((dec_seq, w), lambda h, c: (seq_of(h, c), 0))
    state_s_spec = pl.BlockSpec((None, RET_HEADS, RET_KEY_DIM, RET_VALUE_DIM), lambda h, c: (seq_of(h, c), 0, 0, 0))
    return pl.pallas_call(
        _retention_kernel,
        grid=(RET_HEADS, nc),
        in_specs=[pl.BlockSpec(memory_space=pltpu.SMEM), tok(RET_KEY_DIM), tok(RET_KEY_DIM), tok(RET_VALUE_DIM),
                  tok(RET_VALUE_DIM), state_s_spec, tok_s(RET_QK_W), tok_s(RET_QK_W), tok_s(RET_V_W), tok_s(RET_V_W)],
        out_specs=[
            tok(RET_VALUE_DIM),
            pl.BlockSpec((batch, None, RET_KEY_DIM, RET_VALUE_DIM), lambda h, c: (0, h, 0, 0)),
            tok_s(RET_V_W),
            state_s_spec,
        ],
        out_shape=[
            jax.ShapeDtypeStruct((batch, seq, RET_V_W), BF16),
            jax.ShapeDtypeStruct((batch, RET_HEADS, RET_KEY_DIM, RET_VALUE_DIM), F32),
            jax.ShapeDtypeStruct((dec_batch * dec_seq, RET_V_W), F32),
            jax.ShapeDtypeStruct(state_s.shape, F32),
        ],
        scratch_shapes=[pltpu.VMEM((batch, RET_KEY_DIM, RET_VALUE_DIM), F32)],
        compiler_params=_params("arbitrary", "arbitrary"),
        name="retention",
    )(log_gamma, qr, kr, vr, gr, state_s, qr_s, kr_s, vr_s, gr_s)


def _merge_ffn_kernel(h_ref, a_ref, r_ref, ga_ref, gb_ref, bg_ref, wpa_ref, wpb_ref, wo_ref, gffn_ref, wgu_ref,
                      wdn_ref, gfin_ref, y_ref):
    pa = jnp.dot(a_ref[...].astype(BF16), wpa_ref[...], preferred_element_type=F32)
    pb = jnp.dot(r_ref[...].astype(BF16), wpb_ref[...], preferred_element_type=F32)
    m = (_sigmoid(ga_ref[...] + bg_ref[:, :D_MODEL]) * pa + _sigmoid(gb_ref[...] + bg_ref[:, D_MODEL:]) * pb)
    h1 = h_ref[...] + jnp.dot(m.astype(BF16), wo_ref[...], preferred_element_type=F32)
    xn = (_rms(h1) * gffn_ref[...]).astype(BF16)
    gu = jnp.dot(xn, wgu_ref[...], preferred_element_type=F32)
    g, u = gu[:, :FFN_HIDDEN], gu[:, FFN_HIDDEN:]
    hid = (g * _sigmoid(g) * u).astype(BF16)
    h2 = h1 + jnp.dot(hid, wdn_ref[...], preferred_element_type=F32)
    y_ref[...] = _rms(h2) * gfin_ref[...]


def _merge_ffn(h2d, a, r, ga, gb, b_gate, w_pa, w_pb, w_o, g_ffn, w_gu, w_dn, g_fin, tm):
    rows = h2d.shape[0]
    row = lambda w: pl.BlockSpec((tm, w), lambda i: (i, 0))
    return pl.pallas_call(
        _merge_ffn_kernel,
        grid=(rows // tm,),
        in_specs=[
            row(D_MODEL), row(DIFF_V_W), row(RET_V_W), row(D_MODEL), row(D_MODEL),
            _resident((1, 2 * D_MODEL)),
            _resident((DIFF_V_W, D_MODEL)), _resident((RET_V_W, D_MODEL)), _resident((D_MODEL, D_MODEL)),
            _resident((1, D_MODEL)),
            _resident((D_MODEL, 2 * FFN_HIDDEN)), _resident((FFN_HIDDEN, D_MODEL)),
            _resident((1, D_MODEL)),
        ],
        out_specs=row(D_MODEL),
        out_shape=jax.ShapeDtypeStruct((rows, D_MODEL), F32),
        compiler_params=_params("parallel"),
        name="merge_ffn",
    )(h2d, a, r, ga, gb, b_gate.reshape(1, -1), w_pa, w_pb, w_o, g_ffn.reshape(1, -1), w_gu, w_dn,
      g_fin.reshape(1, -1))


def kernel(x_prompt, x_sample, cache_k, cache_v, state_ret, page_table, norm_mix_g, w_in, b_gate, lambda_q1, lambda_k1,
           lambda_q2, lambda_k2, subln_g, w_pa, w_pb, w_o, norm_ffn_g, w_gu, w_down, norm_final_g):
    batch, seq, _ = x_prompt.shape
    dec_batch, dec_seq, _ = x_sample.shape
    depth, n_phys, page = cache_k.shape[:3]
    assert depth == 1, "the final norm is fused into the (single) layer's FFN kernel"
    past = page_table.shape[1] * page
    tm = 256

    log_gamma = jnp.log(1.0 - jnp.exp2(-5.0 - jnp.arange(RET_HEADS, dtype=F32)))
    tab_p = jnp.asarray(_rotation_tables(np.arange(seq)))
    tab_s = jnp.asarray(np.tile(_rotation_tables(np.arange(dec_seq) + past), (tm // dec_seq, 1)))

    l = 0
    lam_init = _lam_init(l)
    lams = jnp.stack([lambda_q1[l], lambda_k1[l], lambda_q2[l], lambda_k2[l]]).astype(F32)
    subln = subln_g[l].reshape(1, DIFF_V_DIM).astype(F32)
    w_in_b = w_in[l].astype(BF16)
    weights = (b_gate[l], w_pa[l].astype(BF16), w_pb[l].astype(BF16), w_o[l].astype(BF16), norm_ffn_g[l],
               w_gu[l].astype(BF16), w_down[l].astype(BF16), norm_final_g)
    kt_pages = jnp.transpose(cache_k[l], (0, 2, 3, 4, 1)).reshape(n_phys, DIFF_QK_W, page)
    v_pages = cache_v[l].reshape(n_phys, page * DIFF_HEADS, DIFF_V_DIM)

    hp = x_prompt.reshape(batch * seq, D_MODEL)
    hs = x_sample.reshape(dec_batch * dec_seq, D_MODEL)
    qx, kn, kt_p, vt, v_p, qr_p, kr_p, vr_p, gr_p, ga_p, gb_p = _project_prompt(
        hp, norm_mix_g[l], w_in_b, tab_p, batch, seq, tm)
    qa_s, ka_s, va_s, qr_s, kr_s, vr_s, gr_s, ga_s, gb_s = _project_sample(hs, norm_mix_g[l], w_in_b, tab_s, tm)
    a_p, a_s = _attention(page_table, lams, subln, qx, kn, vt, qa_s, ka_s, va_s, kt_pages, v_pages, batch, seq,
                          dec_seq, lam_init)

    as_seq = lambda x: x.reshape(batch, seq, x.shape[-1])
    r_p, s_p, r_s, s_s = _retention(log_gamma, as_seq(qr_p), as_seq(kr_p), as_seq(vr_p), as_seq(gr_p), state_ret[l],
                                    qr_s, kr_s, vr_s, gr_s, batch, seq, dec_seq)
    y_p = _merge_ffn(hp, a_p, r_p.reshape(batch * seq, RET_V_W), ga_p, gb_p, *weights, tm)
    y_s = _merge_ffn(hs, a_s, r_s, ga_s, gb_s, *weights, tm)

    k_p = jnp.transpose(kt_p.reshape(1, batch, DIFF_HEADS, 2, DIFF_HEAD_DIM, seq), (0, 1, 5, 2, 3, 4))
    return (
        y_p.reshape(batch, seq, D_MODEL),
        y_s.reshape(dec_batch, dec_seq, D_MODEL),
        k_p,
        v_p.reshape(1, batch, seq, DIFF_HEADS, DIFF_V_DIM),
        s_p[None],
        ka_s.reshape(1, dec_batch, dec_seq, DIFF_HEADS, 2, DIFF_HEAD_DIM),
        va_s.reshape(1, dec_batch, dec_seq, DIFF_HEADS, DIFF_V_DIM),
        s_s[None],
    )
```

```python
import functools
import math

import jax
import jax.numpy as jnp
import numpy as np
from jax import lax
from jax.experimental import pallas as pl
from jax.experimental.pallas import tpu as pltpu

F32 = jnp.float32
BF16 = jnp.bfloat16

D_MODEL = 1024
DIFF_HEADS = 8
DIFF_HEAD_DIM = 64
DIFF_V_DIM = 2 * DIFF_HEAD_DIM
ROT_DIM = DIFF_HEAD_DIM // 4
ROPE_THETA = 500000.0
RET_HEADS = 4
RET_KEY_DIM = 256
RET_VALUE_DIM = 512
RET_THETA = 10000.0
RET_CHUNK = 128
FFN_HIDDEN = 2816
NORM_EPS = 1e-5
NEG_INF = -1e30

DIFF_QK_W = DIFF_HEADS * 2 * DIFF_HEAD_DIM
DIFF_V_W = DIFF_HEADS * DIFF_V_DIM
RET_QK_W = RET_HEADS * RET_KEY_DIM
RET_V_W = RET_HEADS * RET_VALUE_DIM
IN_W = 3 * DIFF_QK_W + 2 * RET_QK_W + 2 * RET_V_W + 2 * D_MODEL

LANES = 128
ROPE_TAB_W = 3 * LANES
RET_TAB_W = 3 * RET_KEY_DIM
TAB_W = ROPE_TAB_W + RET_TAB_W
VMEM_LIMIT = 56 * 1024 * 1024
BF16_SUBLANES = 16
ACC_ROWS = DIFF_V_DIM + BF16_SUBLANES
LOG2E = math.log2(math.e)


def _lam_init(layer):
    return 0.8 - 0.6 * math.exp(-0.3 * layer)


def _sigmoid(x):
    return 1.0 / (1.0 + jnp.exp(-x))


def _rms(x):
    return x * lax.rsqrt(jnp.mean(x * x, axis=-1, keepdims=True) + NORM_EPS)


def _resident(shape):
    return pl.BlockSpec(shape, lambda *_: (0,) * len(shape), pipeline_mode=pl.Buffered(1))


def _params(*semantics):
    return pltpu.CompilerParams(dimension_semantics=semantics, vmem_limit_bytes=VMEM_LIMIT)


def _rotation_tables(pos):
    pos = np.asarray(pos, np.float64)
    half = ROT_DIM // 2
    freqs = np.exp(-math.log(ROPE_THETA) * np.arange(half) * (2.0 / ROT_DIM))
    ang = pos[:, None] * freqs[None, :]
    cos, sin = np.cos(ang), np.sin(ang)
    lane = np.arange(LANES) % DIFF_HEAD_DIM
    idx = lane % half
    rope_c = np.where(lane < ROT_DIM, cos[:, idx], 1.0)
    rope_s1 = np.where(lane < half, -sin[:, idx], 0.0)
    rope_s2 = np.where((lane >= half) & (lane < ROT_DIM), sin[:, idx], 0.0)

    rhalf = RET_KEY_DIM // 2
    angle = np.exp(-math.log(RET_THETA) * np.linspace(0.0, 1.0, rhalf))
    rang = pos[:, None] * angle[None, :]
    rcos, rsin = np.cos(rang), np.sin(rang)
    lane2 = np.arange(RET_KEY_DIM)
    pair = lane2 // 2
    even = lane2 % 2 == 0
    ret_c = rcos[:, pair]
    ret_s1 = np.where(even, -rsin[:, pair], 0.0)
    ret_s2 = np.where(~even, rsin[:, pair], 0.0)
    return np.concatenate([rope_c, rope_s1, rope_s2, ret_c, ret_s1, ret_s2], axis=1).astype(np.float32)


def _partial_rope(acc, tab):
    c, s1, s2 = tab[:, 0:LANES], tab[:, LANES:2 * LANES], tab[:, 2 * LANES:3 * LANES]
    half = ROT_DIM // 2
    outs = []
    for g in range(acc.shape[1] // LANES):
        x = acc[:, g * LANES:(g + 1) * LANES]
        outs.append(x * c + pltpu.roll(x, LANES - half, 1) * s1 + pltpu.roll(x, half, 1) * s2)
    return jnp.concatenate(outs, axis=1)


def _ret_rotate(acc, tab):
    per_head = RET_KEY_DIM // LANES
    outs = []
    for g in range(acc.shape[1] // LANES):
        o = (g % per_head) * LANES
        c = tab[:, o:o + LANES]
        s1 = tab[:, RET_KEY_DIM + o:RET_KEY_DIM + o + LANES]
        s2 = tab[:, 2 * RET_KEY_DIM + o:2 * RET_KEY_DIM + o + LANES]
        x = acc[:, g * LANES:(g + 1) * LANES]
        outs.append(x * c + pltpu.roll(x, LANES - 1, 1) * s1 + pltpu.roll(x, 1, 1) * s2)
    return jnp.concatenate(outs, axis=1)


class _Projector:
    def __init__(self, x_ref, g_ref, w_ref, tab_ref, xn_ref):
        xn_ref[...] = (_rms(x_ref[...]) * g_ref[...]).astype(BF16)
        self._xn_ref, self._w_ref = xn_ref, w_ref
        self._rope_tab = tab_ref[:, 0:ROPE_TAB_W]
        self._ret_tab = tab_ref[:, ROPE_TAB_W:TAB_W]
        self._col = 0

    def _mm(self, n):
        c0, self._col = self._col, self._col + n
        return jnp.dot(self._xn_ref[...], self._w_ref[:, c0:c0 + n], preferred_element_type=F32)

    def qa(self):
        return _partial_rope(self._mm(DIFF_QK_W), self._rope_tab) * (DIFF_HEAD_DIM ** -0.5)

    def ka(self):
        return _partial_rope(self._mm(DIFF_QK_W), self._rope_tab)

    def va(self):
        return self._mm(DIFF_V_W)

    def qr(self):
        return _ret_rotate(self._mm(RET_QK_W), self._ret_tab)

    def kr(self):
        return _ret_rotate(self._mm(RET_QK_W), self._ret_tab) * (RET_KEY_DIM ** -0.5)

    def wide(self):
        return self._mm(RET_V_W)

    def gate(self):
        return self._mm(D_MODEL)


def _proj_sample_kernel(x_ref, g_ref, w_ref, tab_ref, qa_ref, ka_ref, va_ref, qr_ref, kr_ref, vr_ref, gr_ref, ga_ref,
                        gb_ref, xn_ref):
    p = _Projector(x_ref, g_ref, w_ref, tab_ref, xn_ref)
    qa_ref[...] = p.qa()
    ka_ref[...] = p.ka()
    va_ref[...] = p.va()
    qr_ref[...] = p.qr()
    kr_ref[...] = p.kr()
    vr_ref[...] = p.wide()
    gr_ref[...] = p.wide()
    ga_ref[...] = p.gate()
    gb_ref[...] = p.gate()


def _proj_prompt_kernel(x_ref, g_ref, w_ref, tab_ref, qx_ref, kn_ref, kt_ref, vt_ref, vo_ref, qr_ref, kr_ref, vr_ref,
                        gr_ref, ga_ref, gb_ref, xn_ref):
    p = _Projector(x_ref, g_ref, w_ref, tab_ref, xn_ref)
    tm = x_ref.shape[0]
    qt = (p.qa() * LOG2E).T.astype(BF16)
    zeros = jnp.zeros((DIFF_HEAD_DIM, tm), BF16)
    for hc in range(2 * DIFF_HEADS):
        comp = hc % 2
        src = qt[hc * DIFF_HEAD_DIM:(hc + 1) * DIFF_HEAD_DIM]
        lo, hi = (src, zeros) if comp == 0 else (zeros, src)
        qx_ref[hc * DIFF_V_DIM:hc * DIFF_V_DIM + DIFF_HEAD_DIM, :] = lo
        qx_ref[hc * DIFF_V_DIM + DIFF_HEAD_DIM:(hc + 1) * DIFF_V_DIM, :] = hi
    k = p.ka()
    kn_ref[...] = k.astype(BF16)
    kt_ref[...] = k.T
    v = p.va()
    vt_ref[...] = v.T.astype(BF16)
    for h in range(DIFF_HEADS):
        vo_ref[pl.ds(h, tm, stride=DIFF_HEADS), :] = v[:, h * DIFF_V_DIM:(h + 1) * DIFF_V_DIM]
    qr_ref[...] = p.qr().astype(BF16)
    kr_ref[...] = p.kr().astype(BF16)
    vr_ref[...] = p.wide().astype(BF16)
    gr_ref[...] = p.wide()
    ga_ref[...] = p.gate()
    gb_ref[...] = p.gate()


def _proj_in_specs(tm, tab_blocks):
    return [
        pl.BlockSpec((tm, D_MODEL), lambda i: (i, 0)),
        _resident((1, D_MODEL)),
        _resident((D_MODEL, IN_W)),
        pl.BlockSpec((tm, TAB_W), lambda i: (i % tab_blocks, 0)),
    ]


def _project_sample(x2d, gain, w_bf16, tab, tm):
    rows = x2d.shape[0]
    widths = (DIFF_QK_W, DIFF_QK_W, DIFF_V_W, RET_QK_W, RET_QK_W, RET_V_W, RET_V_W, D_MODEL, D_MODEL)
    return pl.pallas_call(
        _proj_sample_kernel,
        grid=(rows // tm,),
        in_specs=_proj_in_specs(tm, tab.shape[0] // tm),
        out_specs=[pl.BlockSpec((tm, w), lambda i: (i, 0)) for w in widths],
        out_shape=[jax.ShapeDtypeStruct((rows, w), F32) for w in widths],
        scratch_shapes=[pltpu.VMEM((tm, D_MODEL), BF16)],
        compiler_params=_params("parallel"),
        name="proj_sample",
    )(x2d, gain.reshape(1, D_MODEL), w_bf16, tab)


def _project_prompt(x2d, gain, w_bf16, tab, batch, seq, tm):
    rows = batch * seq
    per_b = seq // tm
    row = lambda w: pl.BlockSpec((tm, w), lambda i: (i, 0))
    transposed = lambda h: pl.BlockSpec((None, h, tm), lambda i: (i // per_b, 0, i % per_b))
    specs = [
        (transposed(2 * DIFF_QK_W), (batch, 2 * DIFF_QK_W, seq), BF16),
        (row(DIFF_QK_W), (rows, DIFF_QK_W), BF16),
        (transposed(DIFF_QK_W), (batch, DIFF_QK_W, seq), F32),
        (transposed(DIFF_V_W), (batch, DIFF_V_W, seq), BF16),
        (pl.BlockSpec((tm * DIFF_HEADS, DIFF_V_DIM), lambda i: (i, 0)), (rows * DIFF_HEADS, DIFF_V_DIM), F32),
        (row(RET_QK_W), (rows, RET_QK_W), BF16),
        (row(RET_QK_W), (rows, RET_QK_W), BF16),
        (row(RET_V_W), (rows, RET_V_W), BF16),
        (row(RET_V_W), (rows, RET_V_W), F32),
        (row(D_MODEL), (rows, D_MODEL), F32),
        (row(D_MODEL), (rows, D_MODEL), F32),
    ]
    return pl.pallas_call(
        _proj_prompt_kernel,
        grid=(rows // tm,),
        in_specs=_proj_in_specs(tm, tab.shape[0] // tm),
        out_specs=[s for s, _, _ in specs],
        out_shape=[jax.ShapeDtypeStruct(shape, dtype) for _, shape, dtype in specs],
        scratch_shapes=[pltpu.VMEM((tm, D_MODEL), BF16)],
        compiler_params=_params("parallel"),
        name="proj_prompt",
    )(x2d, gain.reshape(1, D_MODEL), w_bf16, tab)


def _lambda(lam_ref, lam_init):
    lv = lam_ref[...]
    s1 = jnp.sum(lv[0:1] * lv[1:2], axis=-1, keepdims=True)
    s2 = jnp.sum(lv[2:3] * lv[3:4], axis=-1, keepdims=True)
    return jnp.exp(s1) - jnp.exp(s2) + lam_init


def _dot_nt(a, b):
    return lax.dot_general(a, b, (((1,), (1,)), ((), ())), preferred_element_type=F32)


def _sub_layer_norm(o, sub_ref, lam_init):
    return _rms(o) * sub_ref[...] * (1.0 - lam_init)


def _attn_kernel(pt_ref, qi_ref, ki_ref, seq_ref, dec_ref, lam_ref, sub_ref, qx_ref, k_ref, vt_ref, qs_ref, kn_ref,
                 vn_ref, *rest, n_pages, lam_init):
    del pt_ref, seq_ref
    kt_pages = rest[:n_pages]
    v_pages = rest[n_pages:2 * n_pages]
    o_ref, os_ref, m_ref, acc_ref, st_ref, mx_ref = rest[2 * n_pages:]
    qi = qi_ref[pl.program_id(1)]
    ki = ki_ref[pl.program_id(1)]
    has_decode = dec_ref[pl.program_id(0) * pl.num_programs(1) + pl.program_id(1)] == 1
    n_stat = 2 * DIFF_HEADS
    tk = k_ref.shape[0]
    lam = _lambda(lam_ref, lam_init)

    @pl.when(ki == 0)
    def _():
        m_ref[...] = jnp.full(m_ref.shape, NEG_INF, F32)
        acc_ref[...] = jnp.zeros(acc_ref.shape, F32)

    half = tk // 2

    def causal(st):
        key = lax.broadcasted_iota(jnp.int32, st.shape, 0)
        qry = lax.broadcasted_iota(jnp.int32, st.shape, 1)
        return jnp.where(key <= qry, st, NEG_INF)

    def scores(i, on_diagonal):
        h = i // 2
        hs = slice(h * DIFF_V_DIM, (h + 1) * DIFF_V_DIM)
        qs = slice(i * DIFF_V_DIM, (i + 1) * DIFF_V_DIM)
        if not on_diagonal:
            st = jnp.dot(k_ref[:, hs], qx_ref[qs, :], preferred_element_type=F32)
            st_ref[i % 2] = st
            mx_ref[i % 2] = jnp.max(st, axis=0, keepdims=True)
            return
        top = causal(jnp.dot(k_ref[0:half, hs], qx_ref[qs, :], preferred_element_type=F32))
        low = causal(jnp.dot(k_ref[half:tk, hs], qx_ref[qs, half:tk], preferred_element_type=F32))
        st_ref[i % 2, 0:half, :] = top
        st_ref[i % 2, half:tk, half:tk] = low
        top_max = jnp.max(top, axis=0, keepdims=True)
        low_max = jnp.max(low, axis=0, keepdims=True)
        mx_ref[i % 2] = jnp.concatenate([top_max[:, 0:half], jnp.maximum(top_max[:, half:tk], low_max)], axis=1)

    def weighted_values(i, vt_h, m_new, on_diagonal):
        if not on_diagonal:
            p = jnp.exp2(st_ref[i % 2] - m_new).astype(BF16)
            return jnp.dot(vt_h, p, preferred_element_type=F32)
        p_top = jnp.exp2(st_ref[i % 2, 0:half, :] - m_new).astype(BF16)
        p_low = jnp.exp2(st_ref[i % 2, half:tk, half:tk] - m_new[:, half:tk]).astype(BF16)
        upd = jnp.dot(vt_h[:, 0:half], p_top, preferred_element_type=F32)
        low = jnp.dot(vt_h[:, half:tk], p_low, preferred_element_type=F32)
        return jnp.concatenate([upd[:, 0:half], upd[:, half:tk] + low], axis=1)

    def step(on_diagonal, with_decode):
        ones = jnp.ones((ACC_ROWS - DIFF_V_DIM, tk), BF16)
        probs = []
        scores(0, on_diagonal)
        for i in range(n_stat):
            if i + 1 < n_stat:
                scores(i + 1, on_diagonal)
            h = i // 2
            vt_h = jnp.concatenate([vt_ref[h * DIFF_V_DIM:(h + 1) * DIFF_V_DIM, :], ones], axis=0)
            m_prev = m_ref[i:i + 1, :]
            m_new = jnp.maximum(m_prev, mx_ref[i % 2])
            acc_ref[i] = jnp.exp2(m_prev - m_new) * acc_ref[i] + weighted_values(i, vt_h, m_new, on_diagonal)
            m_ref[i:i + 1, :] = m_new
            if with_decode and i % 2 == 1:
                probs.append(_sample_probs(h, qs_ref, kn_ref, kt_pages))
            if with_decode and i % 4 == 3:
                _sample_values(h - 1, probs[h - 1:h + 1], lam, sub_ref, vn_ref, v_pages, os_ref, lam_init)

    def finish_q_block():
        for h in range(DIFF_HEADS):
            o1, o2 = (acc_ref[i, 0:DIFF_V_DIM, :] / acc_ref[i, DIFF_V_DIM:DIFF_V_DIM + 1, :] for i in (2 * h, 2 * h + 1))
            ot = o1 - lam * o2
            o_ref[:, h * DIFF_V_DIM:(h + 1) * DIFF_V_DIM] = _sub_layer_norm(ot.T, sub_ref, lam_init).astype(o_ref.dtype)

    @pl.when(ki < qi)
    def _():
        step(False, True)

    @pl.when((ki == qi) & has_decode)
    def _():
        step(True, True)
        finish_q_block()

    @pl.when((ki == qi) & jnp.logical_not(has_decode))
    def _():
        step(True, False)
        finish_q_block()


def _attention(page_table, lams, subln, qx, kn, vt, qa_s, ka_s, va_s, kt_pages, v_pages, batch, seq, dec_seq, lam_init,
               blk=512):
    nb = seq // blk
    n_stat = 2 * DIFF_HEADS
    dec_batch, n_pages = page_table.shape
    page = kt_pages.shape[2]
    tiles = [(qi, ki) for qi in range(nb) for ki in range(qi + 1)]
    n_tiles = len(tiles)
    flat = [tile for _ in range(batch) for tile in tiles]
    surplus = len(flat) - dec_batch
    assert 0 <= surplus <= batch * nb, "need at least one tile per decode sequence, surplus only on diagonal tiles"
    decode = [True] * len(flat)
    for f in reversed(range(len(flat))):
        if surplus and flat[f][0] == flat[f][1]:
            decode[f] = False
            surplus -= 1
    seq_tab = list(np.maximum(np.cumsum(decode) - 1, 0))
    qi_of = jnp.asarray([qi for qi, _ in tiles], jnp.int32)
    ki_of = jnp.asarray([ki for _, ki in tiles], jnp.int32)
    seq_of = jnp.asarray(seq_tab, jnp.int32)
    dec_of = jnp.asarray(decode, jnp.int32)

    tok_map = lambda b, t, pt, qo, ko, so, do: (so[b * n_tiles + t], 0)

    def page_spec(j):
        return pl.BlockSpec((None, DIFF_QK_W, page),
                            lambda b, t, pt, qo, ko, so, do: (pt[so[b * n_tiles + t] * n_pages + j], 0, 0))

    const = lambda shape: pl.BlockSpec(shape, lambda b, t, *_: (0,) * len(shape))
    return pl.pallas_call(
        functools.partial(_attn_kernel, n_pages=n_pages, lam_init=lam_init),
        grid_spec=pltpu.PrefetchScalarGridSpec(
            num_scalar_prefetch=5,
            grid=(batch, n_tiles),
            in_specs=[
                const((4, DIFF_HEAD_DIM)),
                const((1, DIFF_V_DIM)),
                pl.BlockSpec((None, 2 * DIFF_QK_W, blk), lambda b, t, pt, qo, ko, so, do: (b, 0, qo[t])),
                pl.BlockSpec((blk, DIFF_QK_W), lambda b, t, pt, qo, ko, so, do: (b * nb + ko[t], 0)),
                pl.BlockSpec((None, DIFF_V_W, blk), lambda b, t, pt, qo, ko, so, do: (b, 0, ko[t])),
                pl.BlockSpec((dec_seq, DIFF_QK_W), tok_map),
                pl.BlockSpec((dec_seq, DIFF_QK_W), tok_map),
                pl.BlockSpec((dec_seq, DIFF_V_W), tok_map),
            ] + [page_spec(j) for j in range(n_pages)] * 2,
            out_specs=[
                pl.BlockSpec((blk, DIFF_V_W), lambda b, t, pt, qo, ko, so, do: (b * nb + qo[t], 0)),
                pl.BlockSpec((dec_seq, DIFF_V_W), tok_map),
            ],
            scratch_shapes=[
                pltpu.VMEM((n_stat, blk), F32),
                pltpu.VMEM((n_stat, ACC_ROWS, blk), F32),
                pltpu.VMEM((2, blk, blk), F32),
                pltpu.VMEM((2, 1, blk), F32),
            ],
        ),
        out_shape=[
            jax.ShapeDtypeStruct((batch * seq, DIFF_V_W), BF16),
            jax.ShapeDtypeStruct((dec_batch * dec_seq, DIFF_V_W), F32),
        ],
        compiler_params=_params("arbitrary", "arbitrary"),
        name="attention",
    )(page_table.reshape(-1), qi_of, ki_of, seq_of, dec_of, lams, subln, qx, kn, vt, qa_s, ka_s, va_s,
      *([kt_pages] * n_pages), *([v_pages] * n_pages))


def _sample_probs(h, q_ref, kn_ref, kt_pages):
    t = q_ref.shape[0]
    page = kt_pages[0].shape[1]
    hs = slice(h * DIFF_V_DIM, (h + 1) * DIFF_V_DIM)
    pad = jnp.zeros((page - t, DIFF_V_DIM), BF16)
    row = lax.broadcasted_iota(jnp.int32, (2 * t, page), 0) % t
    col = lax.broadcasted_iota(jnp.int32, (2 * t, page), 1)
    lane = lax.broadcasted_iota(jnp.int32, (t, DIFF_V_DIM), 1)
    qh = q_ref[:, hs].astype(BF16)
    zero = jnp.zeros_like(qh)
    q = jnp.concatenate([jnp.where(lane < DIFF_HEAD_DIM, qh, zero), jnp.where(lane >= DIFF_HEAD_DIM, qh, zero)], axis=0)
    kt = jnp.concatenate([kp[hs, :] for kp in kt_pages], axis=1).astype(BF16)
    s_cache = jnp.dot(q, kt, preferred_element_type=F32)
    k_new = jnp.concatenate([kn_ref[:, hs].astype(BF16), pad], axis=0)
    s_new = jnp.where(col <= row, _dot_nt(q, k_new), NEG_INF)
    s = jnp.concatenate([s_cache, s_new], axis=1)
    p = jnp.exp(s - jnp.max(s, axis=1, keepdims=True))
    return p.astype(BF16), jnp.sum(p, axis=1, keepdims=True)


def _sample_values(h0, probs, lam, sub_ref, vn_ref, v_pages, o_ref, lam_init):
    (p0, l0), (p1, l1) = probs
    t = vn_ref.shape[0]
    page = v_pages[0].shape[0] // DIFF_HEADS
    p = jnp.concatenate([p0, p1], axis=0)
    pad = jnp.zeros((page - t, 2 * DIFF_V_DIM), BF16)
    v_new = jnp.concatenate([vn_ref[:, h0 * DIFF_V_DIM:(h0 + 2) * DIFF_V_DIM].astype(BF16), pad], axis=0)
    v_cache = [
        jnp.concatenate([vp[pl.ds(h0 + d, page, stride=DIFF_HEADS), :] for d in (0, 1)], axis=1).astype(BF16)
        for vp in v_pages]
    acc = jnp.dot(p, jnp.concatenate(v_cache + [v_new], axis=0), preferred_element_type=F32)
    for d, l in ((0, l0), (1, l1)):
        o = acc[2 * t * d:2 * t * (d + 1), d * DIFF_V_DIM:(d + 1) * DIFF_V_DIM] / l
        hs = slice((h0 + d) * DIFF_V_DIM, (h0 + d + 1) * DIFF_V_DIM)
        o_ref[:, hs] = _sub_layer_norm(o[:t] - lam * o[t:], sub_ref, lam_init).astype(o_ref.dtype)


def _iota_f32(shape, axis):
    return lax.broadcasted_iota(jnp.int32, shape, axis).astype(F32)


def _ret_decays(c, dk, dv, lg):
    rel = _iota_f32((c, c), 0) - _iota_f32((c, c), 1)
    decay = jnp.where(rel >= 0, jnp.exp(lg * jnp.maximum(rel, 0.0)), 0.0)
    q_dec = jnp.exp((_iota_f32((c, dk), 0) + 1.0) * lg)
    k_dec = jnp.exp((c - 1.0 - _iota_f32((c, dk), 0)) * lg)
    s_dec = jnp.exp(jnp.full((1, dv), float(c), F32) * lg)
    return decay, q_dec, k_dec, s_dec


def _ret_chunk(s_prev, q, k, v, decays):
    decay, q_dec, k_dec, s_dec = decays
    qk = _dot_nt(q.astype(BF16), k.astype(BF16)) * decay
    o = jnp.dot(qk.astype(BF16), v, preferred_element_type=F32)
    o = o + jnp.dot((q * q_dec).astype(BF16), s_prev.astype(BF16), preferred_element_type=F32)
    kv = jnp.dot((k * k_dec).T.astype(BF16), v, preferred_element_type=F32)
    return s_dec * s_prev + kv, o


def _gated_ret_out(o, gr):
    return _rms(o) * (gr * _sigmoid(gr))


def _retention_kernel(lg_ref, q_ref, k_ref, v_ref, gr_ref, ss_ref, qs_ref, ks_ref, vs_ref, grs_ref, r_ref, state_ref,
                      rs_ref, sso_ref, s_ref):
    c = pl.program_id(1)

    @pl.when(c == 0)
    def _():
        s_ref[...] = jnp.zeros(s_ref.shape, F32)

    n_stages = 4

    def decode_heads(stage):
        t = qs_ref.shape[0]
        for h in range(stage, RET_HEADS, n_stages):
            ks = slice(h * RET_KEY_DIM, (h + 1) * RET_KEY_DIM)
            vs = slice(h * RET_VALUE_DIM, (h + 1) * RET_VALUE_DIM)
            decays = _ret_decays(t, RET_KEY_DIM, RET_VALUE_DIM, lg_ref[h])
            s_new, o = _ret_chunk(ss_ref[h], qs_ref[:, ks], ks_ref[:, ks], vs_ref[:, vs].astype(BF16), decays)
            sso_ref[h] = s_new
            rs_ref[:, vs] = _gated_ret_out(o, grs_ref[:, vs]).astype(rs_ref.dtype)

    decay, q_dec, k_dec, s_dec = _ret_decays(RET_CHUNK, RET_KEY_DIM, RET_VALUE_DIM, lg_ref[pl.program_id(0)])
    batch = range(q_ref.shape[0])
    qk = [(_dot_nt(q_ref[b], k_ref[b]) * decay).astype(BF16) for b in batch]
    o = [jnp.dot(qk[b], v_ref[b], preferred_element_type=F32) for b in batch]
    decode_heads(0)
    for b in batch:
        qd = (q_ref[b].astype(F32) * q_dec).astype(BF16)
        o[b] = o[b] + jnp.dot(qd, s_ref[b].astype(BF16), preferred_element_type=F32)
    decode_heads(1)
    for b in batch:
        kd = (k_ref[b].astype(F32) * k_dec).T.astype(BF16)
        s_ref[b] = s_dec * s_ref[b] + jnp.dot(kd, v_ref[b], preferred_element_type=F32)
    decode_heads(2)
    for b in batch:
        r_ref[b] = _gated_ret_out(o[b], gr_ref[b]).astype(r_ref.dtype)
    decode_heads(3)

    @pl.when(c == pl.num_programs(1) - 1)
    def _():
        state_ref[...] = s_ref[...]


def _retention(log_gamma, qr, kr, vr, gr, state_s, qr_s, kr_s, vr_s, gr_s, batch, seq, dec_seq):
    nc = seq // RET_CHUNK
    dec_batch = state_s.shape[0]
    assert RET_HEADS * nc >= dec_batch
    seq_of = lambda h, c: jnp.minimum(h * nc + c, dec_batch - 1)
    tok = lambda w: pl.BlockSpec((batch, RET_CHUNK, w), lambda h, c: (0, c, h))
    tok_s = lambda w: pl.BlockSpec((dec_seq, w), lambda h, c: (seq_of(h, c), 0))
    state_s_spec = pl.BlockSpec((None, RET_HEADS, RET_KEY_DIM, RET_VALUE_DIM), lambda h, c: (seq_of(h, c), 0, 0, 0))
    return pl.pallas_call(
        _retention_kernel,
        grid=(RET_HEADS, nc),
        in_specs=[pl.BlockSpec(memory_space=pltpu.SMEM), tok(RET_KEY_DIM), tok(RET_KEY_DIM), tok(RET_VALUE_DIM),
                  tok(RET_VALUE_DIM), state_s_spec, tok_s(RET_QK_W), tok_s(RET_QK_W), tok_s(RET_V_W), tok_s(RET_V_W)],
        out_specs=[
            tok(RET_VALUE_DIM),
            pl.BlockSpec((batch, None, RET_KEY_DIM, RET_VALUE_DIM), lambda h, c: (0, h, 0, 0)),
            tok_s(RET_V_W),
            state_s_spec,
        ],
        out_shape=[
            jax.ShapeDtypeStruct((batch, seq, RET_V_W), BF16),
            jax.ShapeDtypeStruct((batch, RET_HEADS, RET_KEY_DIM, RET_VALUE_DIM), F32),
            jax.ShapeDtypeStruct((dec_batch * dec_seq, RET_V_W), F32),
            jax.ShapeDtypeStruct(state_s.shape, F32),
        ],
        scratch_shapes=[pltpu.VMEM((batch, RET_KEY_DIM, RET_VALUE_DIM), F32)],
        compiler_params=_params("arbitrary", "arbitrary"),
        name="retention",
    )(log_gamma, qr, kr, vr, gr, state_s, qr_s, kr_s, vr_s, gr_s)


def _merge_ffn_kernel(h_ref, a_ref, r_ref, ga_ref, gb_ref, bg_ref, wpa_ref, wpb_ref, wo_ref, gffn_ref, wgu_ref,
                      wdn_ref, gfin_ref, y_ref):
    pa = jnp.dot(a_ref[...].astype(BF16), wpa_ref[...], preferred_element_type=F32)
    pb = jnp.dot(r_ref[...].astype(BF16), wpb_ref[...], preferred_element_type=F32)
    m = (_sigmoid(ga_ref[...] + bg_ref[:, :D_MODEL]) * pa + _sigmoid(gb_ref[...] + bg_ref[:, D_MODEL:]) * pb)
    h1 = h_ref[...] + jnp.dot(m.astype(BF16), wo_ref[...], preferred_element_type=F32)
    xn = (_rms(h1) * gffn_ref[...]).astype(BF16)
    gu = jnp.dot(xn, wgu_ref[...], preferred_element_type=F32)
    g, u = gu[:, :FFN_HIDDEN], gu[:, FFN_HIDDEN:]
    hid = (g * _sigmoid(g) * u).astype(BF16)
    h2 = h1 + jnp.dot(hid, wdn_ref[...], preferred_element_type=F32)
    y_ref[...] = _rms(h2) * gfin_ref[...]


def _merge_ffn(h2d, a, r, ga, gb, b_gate, w_pa, w_pb, w_o, g_ffn, w_gu, w_dn, g_fin, tm):
    rows = h2d.shape[0]
    row = lambda w: pl.BlockSpec((tm, w), lambda i: (i, 0))
    return pl.pallas_call(
        _merge_ffn_kernel,
        grid=(rows // tm,),
        in_specs=[
            row(D_MODEL), row(DIFF_V_W), row(RET_V_W), row(D_MODEL), row(D_MODEL),
            _resident((1, 2 * D_MODEL)),
            _resident((DIFF_V_W, D_MODEL)), _resident((RET_V_W, D_MODEL)), _resident((D_MODEL, D_MODEL)),
            _resident((1, D_MODEL)),
            _resident((D_MODEL, 2 * FFN_HIDDEN)), _resident((FFN_HIDDEN, D_MODEL)),
            _resident((1, D_MODEL)),
        ],
        out_specs=row(D_MODEL),
        out_shape=jax.ShapeDtypeStruct((rows, D_MODEL), F32),
        compiler_params=_params("parallel"),
        name="merge_ffn",
    )(h2d, a, r, ga, gb, b_gate.reshape(1, -1), w_pa, w_pb, w_o, g_ffn.reshape(1, -1), w_gu, w_dn,
      g_fin.reshape(1, -1))


def kernel(x_prompt, x_sample, cache_k, cache_v, state_ret, page_table, norm_mix_g, w_in, b_gate, lambda_q1, lambda_k1,
           lambda_q2, lambda_k2, subln_g, w_pa, w_pb, w_o, norm_ffn_g, w_gu, w_down, norm_final_g):
    batch, seq, _ = x_prompt.shape
    dec_batch, dec_seq, _ = x_sample.shape
    depth, n_phys, page = cache_k.shape[:3]
    assert depth == 1, "the final norm is fused into the (single) layer's FFN kernel"
    past = page_table.shape[1] * page
    tm = 256

    log_gamma = jnp.log(1.0 - jnp.exp2(-5.0 - jnp.arange(RET_HEADS, dtype=F32)))
    tab_p = jnp.asarray(_rotation_tables(np.arange(seq)))
    tab_s = jnp.asarray(np.tile(_rotation_tables(np.arange(dec_seq) + past), (tm // dec_seq, 1)))

    l = 0
    lam_init = _lam_init(l)
    lams = jnp.stack([lambda_q1[l], lambda_k1[l], lambda_q2[l], lambda_k2[l]]).astype(F32)
    subln = subln_g[l].reshape(1, DIFF_V_DIM).astype(F32)
    w_in_b = w_in[l].astype(BF16)
    weights = (b_gate[l], w_pa[l].astype(BF16), w_pb[l].astype(BF16), w_o[l].astype(BF16), norm_ffn_g[l],
               w_gu[l].astype(BF16), w_down[l].astype(BF16), norm_final_g)
    kt_pages = jnp.transpose(cache_k[l], (0, 2, 3, 4, 1)).reshape(n_phys, DIFF_QK_W, page)
    v_pages = cache_v[l].reshape(n_phys, page * DIFF_HEADS, DIFF_V_DIM)

    hp = x_prompt.reshape(batch * seq, D_MODEL)
    hs = x_sample.reshape(dec_batch * dec_seq, D_MODEL)
    qx, kn, kt_p, vt, v_p, qr_p, kr_p, vr_p, gr_p, ga_p, gb_p = _project_prompt(
        hp, norm_mix_g[l], w_in_b, tab_p, batch, seq, tm)
    qa_s, ka_s, va_s, qr_s, kr_s, vr_s, gr_s, ga_s, gb_s = _project_sample(hs, norm_mix_g[l], w_in_b, tab_s, tm)
    a_p, a_s = _attention(page_table, lams, subln, qx, kn, vt, qa_s, ka_s, va_s, kt_pages, v_pages, batch, seq,
                          dec_seq, lam_init)

    as_seq = lambda x: x.reshape(batch, seq, x.shape[-1])
    r_p, s_p, r_s, s_s = _retention(log_gamma, as_seq(qr_p), as_seq(kr_p), as_seq(vr_p), as_seq(gr_p), state_ret[l],
                                    qr_s, kr_s, vr_s, gr_s, batch, seq, dec_seq)
    y_p = _merge_ffn(hp, a_p, r_p.reshape(batch * seq, RET_V_W), ga_p, gb_p, *weights, tm)
    y_s = _merge_ffn(hs, a_s, r_s, ga_s, gb_s, *weights, tm)

    k_p = jnp.transpose(kt_p.reshape(1, batch, DIFF_HEADS, 2, DIFF_HEAD_DIM, seq), (0, 1, 5, 2, 3, 4))
    return (
        y_p.reshape(batch, seq, D_MODEL),
        y_s.reshape(dec_batch, dec_seq, D_MODEL),
        k_p,
        v_p.reshape(1, batch, seq, DIFF_HEADS, DIFF_V_DIM),
        s_p[None],
        ka_s.reshape(1, dec_batch, dec_seq, DIFF_HEADS, 2, DIFF_HEAD_DIM),
        va_s.reshape(1, dec_batch, dec_seq, DIFF_HEADS, DIFF_V_DIM),
        s_s[None],
    )
```

```python
import functools
import math

import jax
import jax.numpy as jnp
import numpy as np
from jax import lax
from jax.experimental import pallas as pl
from jax.experimental.pallas import tpu as pltpu

F32 = jnp.float32
BF16 = jnp.bfloat16

D_MODEL = 1024
DIFF_HEADS = 8
DIFF_HEAD_DIM = 64
DIFF_V_DIM = 2 * DIFF_HEAD_DIM
ROT_DIM = DIFF_HEAD_DIM // 4
ROPE_THETA = 500000.0
RET_HEADS = 4
RET_KEY_DIM = 256
RET_VALUE_DIM = 512
RET_THETA = 10000.0
RET_CHUNK = 128
FFN_HIDDEN = 2816
NORM_EPS = 1e-5
NEG_INF = -1e30

DIFF_QK_W = DIFF_HEADS * 2 * DIFF_HEAD_DIM
DIFF_V_W = DIFF_HEADS * DIFF_V_DIM
RET_QK_W = RET_HEADS * RET_KEY_DIM
RET_V_W = RET_HEADS * RET_VALUE_DIM
IN_W = 3 * DIFF_QK_W + 2 * RET_QK_W + 2 * RET_V_W + 2 * D_MODEL

LANES = 128
ROPE_TAB_W = 3 * LANES
RET_TAB_W = 3 * RET_KEY_DIM
TAB_W = ROPE_TAB_W + RET_TAB_W
VMEM_LIMIT = 56 * 1024 * 1024
BF16_SUBLANES = 16
ACC_ROWS = DIFF_V_DIM + BF16_SUBLANES
LOG2E = math.log2(math.e)


def _lam_init(layer):
    return 0.8 - 0.6 * math.exp(-0.3 * layer)


def _sigmoid(x):
    return 1.0 / (1.0 + jnp.exp(-x))


def _rms(x):
    return x * lax.rsqrt(jnp.mean(x * x, axis=-1, keepdims=True) + NORM_EPS)


def _resident(shape):
    return pl.BlockSpec(shape, lambda *_: (0,) * len(shape), pipeline_mode=pl.Buffered(1))


def _params(*semantics):
    return pltpu.CompilerParams(dimension_semantics=semantics, vmem_limit_bytes=VMEM_LIMIT)


def _rotation_tables(pos):
    pos = np.asarray(pos, np.float64)
    half = ROT_DIM // 2
    freqs = np.exp(-math.log(ROPE_THETA) * np.arange(half) * (2.0 / ROT_DIM))
    ang = pos[:, None] * freqs[None, :]
    cos, sin = np.cos(ang), np.sin(ang)
    lane = np.arange(LANES) % DIFF_HEAD_DIM
    idx = lane % half
    rope_c = np.where(lane < ROT_DIM, cos[:, idx], 1.0)
    rope_s1 = np.where(lane < half, -sin[:, idx], 0.0)
    rope_s2 = np.where((lane >= half) & (lane < ROT_DIM), sin[:, idx], 0.0)

    rhalf = RET_KEY_DIM // 2
    angle = np.exp(-math.log(RET_THETA) * np.linspace(0.0, 1.0, rhalf))
    rang = pos[:, None] * angle[None, :]
    rcos, rsin = np.cos(rang), np.sin(rang)
    lane2 = np.arange(RET_KEY_DIM)
    pair = lane2 // 2
    even = lane2 % 2 == 0
    ret_c = rcos[:, pair]
    ret_s1 = np.where(even, -rsin[:, pair], 0.0)
    ret_s2 = np.where(~even, rsin[:, pair], 0.0)
    return np.concatenate([rope_c, rope_s1, rope_s2, ret_c, ret_s1, ret_s2], axis=1).astype(np.float32)


def _partial_rope(acc, tab):
    c, s1, s2 = tab[:, 0:LANES], tab[:, LANES:2 * LANES], tab[:, 2 * LANES:3 * LANES]
    half = ROT_DIM // 2
    outs = []
    for g in range(acc.shape[1] // LANES):
        x = acc[:, g * LANES:(g + 1) * LANES]
        outs.append(x * c + pltpu.roll(x, LANES - half, 1) * s1 + pltpu.roll(x, half, 1) * s2)
    return jnp.concatenate(outs, axis=1)


def _ret_rotate(acc, tab):
    per_head = RET_KEY_DIM // LANES
    outs = []
    for g in range(acc.shape[1] // LANES):
        o = (g % per_head) * LANES
        c = tab[:, o:o + LANES]
        s1 = tab[:, RET_KEY_DIM + o:RET_KEY_DIM + o + LANES]
        s2 = tab[:, 2 * RET_KEY_DIM + o:2 * RET_KEY_DIM + o + LANES]
        x = acc[:, g * LANES:(g + 1) * LANES]
        outs.append(x * c + pltpu.roll(x, LANES - 1, 1) * s1 + pltpu.roll(x, 1, 1) * s2)
    return jnp.concatenate(outs, axis=1)


class _Projector:
    def __init__(self, x_ref, g_ref, w_ref, tab_ref, xn_ref):
        xn_ref[...] = (_rms(x_ref[...]) * g_ref[...]).astype(BF16)
        self._xn_ref, self._w_ref = xn_ref, w_ref
        self._rope_tab = tab_ref[:, 0:ROPE_TAB_W]
        self._ret_tab = tab_ref[:, ROPE_TAB_W:TAB_W]
        self._col = 0

    def _mm(self, n):
        c0, self._col = self._col, self._col + n
        return jnp.dot(self._xn_ref[...], self._w_ref[:, c0:c0 + n], preferred_element_type=F32)

    def qa(self):
        return _partial_rope(self._mm(DIFF_QK_W), self._rope_tab) * (DIFF_HEAD_DIM ** -0.5)

    def ka(self):
        return _partial_rope(self._mm(DIFF_QK_W), self._rope_tab)

    def va(self):
        return self._mm(DIFF_V_W)

    def qr(self):
        return _ret_rotate(self._mm(RET_QK_W), self._ret_tab)

    def kr(self):
        return _ret_rotate(self._mm(RET_QK_W), self._ret_tab) * (RET_KEY_DIM ** -0.5)

    def wide(self):
        return self._mm(RET_V_W)

    def gate(self):
        return self._mm(D_MODEL)


def _proj_sample_kernel(x_ref, g_ref, w_ref, tab_ref, qa_ref, ka_ref, va_ref, qr_ref, kr_ref, vr_ref, gr_ref, ga_ref,
                        gb_ref, xn_ref):
    p = _Projector(x_ref, g_ref, w_ref, tab_ref, xn_ref)
    qa_ref[...] = p.qa()
    ka_ref[...] = p.ka()
    va_ref[...] = p.va()
    qr_ref[...] = p.qr()
    kr_ref[...] = p.kr()
    vr_ref[...] = p.wide()
    gr_ref[...] = p.wide()
    ga_ref[...] = p.gate()
    gb_ref[...] = p.gate()


def _proj_prompt_kernel(x_ref, g_ref, w_ref, tab_ref, qx_ref, kn_ref, kt_ref, vt_ref, vo_ref, qr_ref, kr_ref, vr_ref,
                        gr_ref, ga_ref, gb_ref, xn_ref):
    p = _Projector(x_ref, g_ref, w_ref, tab_ref, xn_ref)
    tm = x_ref.shape[0]
    qt = (p.qa() * LOG2E).T.astype(BF16)
    zeros = jnp.zeros((DIFF_HEAD_DIM, tm), BF16)
    for hc in range(2 * DIFF_HEADS):
        comp = hc % 2
        src = qt[hc * DIFF_HEAD_DIM:(hc + 1) * DIFF_HEAD_DIM]
        lo, hi = (src, zeros) if comp == 0 else (zeros, src)
        qx_ref[hc * DIFF_V_DIM:hc * DIFF_V_DIM + DIFF_HEAD_DIM, :] = lo
        qx_ref[hc * DIFF_V_DIM + DIFF_HEAD_DIM:(hc + 1) * DIFF_V_DIM, :] = hi
    k = p.ka()
    kn_ref[...] = k.astype(BF16)
    kt_ref[...] = k.T
    v = p.va()
    vt_ref[...] = v.T.astype(BF16)
    for h in range(DIFF_HEADS):
        vo_ref[pl.ds(h, tm, stride=DIFF_HEADS), :] = v[:, h * DIFF_V_DIM:(h + 1) * DIFF_V_DIM]
    qr_ref[...] = p.qr().astype(BF16)
    kr_ref[...] = p.kr().astype(BF16)
    vr_ref[...] = p.wide().astype(BF16)
    gr_ref[...] = p.wide().astype(BF16)
    ga_ref[...] = p.gate()
    gb_ref[...] = p.gate()


def _proj_in_specs(tm, tab_blocks):
    return [
        pl.BlockSpec((tm, D_MODEL), lambda i: (i, 0)),
        _resident((1, D_MODEL)),
        _resident((D_MODEL, IN_W)),
        pl.BlockSpec((tm, TAB_W), lambda i: (i % tab_blocks, 0)),
    ]


def _project_sample(x2d, gain, w_bf16, tab, tm):
    rows = x2d.shape[0]
    widths = (DIFF_QK_W, DIFF_QK_W, DIFF_V_W, RET_QK_W, RET_QK_W, RET_V_W, RET_V_W, D_MODEL, D_MODEL)
    return pl.pallas_call(
        _proj_sample_kernel,
        grid=(rows // tm,),
        in_specs=_proj_in_specs(tm, tab.shape[0] // tm),
        out_specs=[pl.BlockSpec((tm, w), lambda i: (i, 0)) for w in widths],
        out_shape=[jax.ShapeDtypeStruct((rows, w), F32) for w in widths],
        scratch_shapes=[pltpu.VMEM((tm, D_MODEL), BF16)],
        compiler_params=_params("parallel"),
        name="proj_sample",
    )(x2d, gain.reshape(1, D_MODEL), w_bf16, tab)


def _project_prompt(x2d, gain, w_bf16, tab, batch, seq, tm):
    rows = batch * seq
    per_b = seq // tm
    row = lambda w: pl.BlockSpec((tm, w), lambda i: (i, 0))
    transposed = lambda h: pl.BlockSpec((None, h, tm), lambda i: (i // per_b, 0, i % per_b))
    specs = [
        (transposed(2 * DIFF_QK_W), (batch, 2 * DIFF_QK_W, seq), BF16),
        (row(DIFF_QK_W), (rows, DIFF_QK_W), BF16),
        (transposed(DIFF_QK_W), (batch, DIFF_QK_W, seq), F32),
        (transposed(DIFF_V_W), (batch, DIFF_V_W, seq), BF16),
        (pl.BlockSpec((tm * DIFF_HEADS, DIFF_V_DIM), lambda i: (i, 0)), (rows * DIFF_HEADS, DIFF_V_DIM), F32),
        (row(RET_QK_W), (rows, RET_QK_W), BF16),
        (row(RET_QK_W), (rows, RET_QK_W), BF16),
        (row(RET_V_W), (rows, RET_V_W), BF16),
        (row(RET_V_W), (rows, RET_V_W), BF16),
        (row(D_MODEL), (rows, D_MODEL), F32),
        (row(D_MODEL), (rows, D_MODEL), F32),
    ]
    return pl.pallas_call(
        _proj_prompt_kernel,
        grid=(rows // tm,),
        in_specs=_proj_in_specs(tm, tab.shape[0] // tm),
        out_specs=[s for s, _, _ in specs],
        out_shape=[jax.ShapeDtypeStruct(shape, dtype) for _, shape, dtype in specs],
        scratch_shapes=[pltpu.VMEM((tm, D_MODEL), BF16)],
        compiler_params=_params("parallel"),
        name="proj_prompt",
    )(x2d, gain.reshape(1, D_MODEL), w_bf16, tab)


def _lambda(lam_ref, lam_init):
    lv = lam_ref[...]
    s1 = jnp.sum(lv[0:1] * lv[1:2], axis=-1, keepdims=True)
    s2 = jnp.sum(lv[2:3] * lv[3:4], axis=-1, keepdims=True)
    return jnp.exp(s1) - jnp.exp(s2) + lam_init


def _dot_nt(a, b):
    return lax.dot_general(a, b, (((1,), (1,)), ((), ())), preferred_element_type=F32)


def _sub_layer_norm(o, sub_ref, lam_init):
    return _rms(o) * sub_ref[...] * (1.0 - lam_init)


def _attn_kernel(pt_ref, qi_ref, ki_ref, seq_ref, dec_ref, lam_ref, sub_ref, qx_ref, k_ref, vt_ref, qs_ref, kn_ref,
                 vn_ref, *rest, n_pages, lam_init):
    del pt_ref, seq_ref
    kt_pages = rest[:n_pages]
    v_pages = rest[n_pages:2 * n_pages]
    o_ref, os_ref, m_ref, acc_ref, st_ref, mx_ref = rest[2 * n_pages:]
    qi = qi_ref[pl.program_id(1)]
    ki = ki_ref[pl.program_id(1)]
    has_decode = dec_ref[pl.program_id(0) * pl.num_programs(1) + pl.program_id(1)] == 1
    n_stat = 2 * DIFF_HEADS
    tk = k_ref.shape[0]
    lam = _lambda(lam_ref, lam_init)

    @pl.when(ki == 0)
    def _():
        m_ref[...] = jnp.full(m_ref.shape, NEG_INF, F32)
        acc_ref[...] = jnp.zeros(acc_ref.shape, F32)

    half = tk // 2

    def causal(st):
        key = lax.broadcasted_iota(jnp.int32, st.shape, 0)
        qry = lax.broadcasted_iota(jnp.int32, st.shape, 1)
        return jnp.where(key <= qry, st, NEG_INF)

    def scores(i, on_diagonal):
        h = i // 2
        hs = slice(h * DIFF_V_DIM, (h + 1) * DIFF_V_DIM)
        qs = slice(i * DIFF_V_DIM, (i + 1) * DIFF_V_DIM)
        if not on_diagonal:
            st = jnp.dot(k_ref[:, hs], qx_ref[qs, :], preferred_element_type=F32)
            st_ref[i % 2] = st
            mx_ref[i % 2] = jnp.max(st, axis=0, keepdims=True)
            return
        top = causal(jnp.dot(k_ref[0:half, hs], qx_ref[qs, :], preferred_element_type=F32))
        low = causal(jnp.dot(k_ref[half:tk, hs], qx_ref[qs, half:tk], preferred_element_type=F32))
        st_ref[i % 2, 0:half, :] = top
        st_ref[i % 2, half:tk, half:tk] = low
        top_max = jnp.max(top, axis=0, keepdims=True)
        low_max = jnp.max(low, axis=0, keepdims=True)
        mx_ref[i % 2] = jnp.concatenate([top_max[:, 0:half], jnp.maximum(top_max[:, half:tk], low_max)], axis=1)

    def weighted_values(i, vt_h, m_new, on_diagonal):
        if not on_diagonal:
            p = jnp.exp2(st_ref[i % 2] - m_new).astype(BF16)
            return jnp.dot(vt_h, p, preferred_element_type=F32)
        p_top = jnp.exp2(st_ref[i % 2, 0:half, :] - m_new).astype(BF16)
        p_low = jnp.exp2(st_ref[i % 2, half:tk, half:tk] - m_new[:, half:tk]).astype(BF16)
        upd = jnp.dot(vt_h[:, 0:half], p_top, preferred_element_type=F32)
        low = jnp.dot(vt_h[:, half:tk], p_low, preferred_element_type=F32)
        return jnp.concatenate([upd[:, 0:half], upd[:, half:tk] + low], axis=1)

    def step(on_diagonal, with_decode):
        ones = jnp.ones((ACC_ROWS - DIFF_V_DIM, tk), BF16)
        probs = []
        scores(0, on_diagonal)
        for i in range(n_stat):
            if i + 1 < n_stat:
                scores(i + 1, on_diagonal)
            h = i // 2
            vt_h = jnp.concatenate([vt_ref[h * DIFF_V_DIM:(h + 1) * DIFF_V_DIM, :], ones], axis=0)
            m_prev = m_ref[i:i + 1, :]
            m_new = jnp.maximum(m_prev, mx_ref[i % 2])
            acc_ref[i] = jnp.exp2(m_prev - m_new) * acc_ref[i] + weighted_values(i, vt_h, m_new, on_diagonal)
            m_ref[i:i + 1, :] = m_new
            if with_decode and i % 2 == 1:
                probs.append(_sample_probs(h, qs_ref, kn_ref, kt_pages))
            if with_decode and i % 4 == 3:
                _sample_values(h - 1, probs[h - 1:h + 1], lam, sub_ref, vn_ref, v_pages, os_ref, lam_init)

    def finish_q_block():
        for h in range(DIFF_HEADS):
            o1, o2 = (acc_ref[i, 0:DIFF_V_DIM, :] / acc_ref[i, DIFF_V_DIM:DIFF_V_DIM + 1, :] for i in (2 * h, 2 * h + 1))
            ot = o1 - lam * o2
            o_ref[:, h * DIFF_V_DIM:(h + 1) * DIFF_V_DIM] = _sub_layer_norm(ot.T, sub_ref, lam_init).astype(o_ref.dtype)

    @pl.when(ki < qi)
    def _():
        step(False, True)

    @pl.when((ki == qi) & has_decode)
    def _():
        step(True, True)
        finish_q_block()

    @pl.when((ki == qi) & jnp.logical_not(has_decode))
    def _():
        step(True, False)
        finish_q_block()


def _attention(page_table, lams, subln, qx, kn, vt, qa_s, ka_s, va_s, kt_pages, v_pages, batch, seq, dec_seq, lam_init,
               blk=512):
    nb = seq // blk
    n_stat = 2 * DIFF_HEADS
    dec_batch, n_pages = page_table.shape
    page = kt_pages.shape[2]
    tiles = [(qi, ki) for qi in range(nb) for ki in range(qi + 1)]
    n_tiles = len(tiles)
    flat = [tile for _ in range(batch) for tile in tiles]
    surplus = len(flat) - dec_batch
    assert 0 <= surplus <= batch * nb, "need at least one tile per decode sequence, surplus only on diagonal tiles"
    decode = [True] * len(flat)
    for f in reversed(range(len(flat))):
        if surplus and flat[f][0] == flat[f][1]:
            decode[f] = False
            surplus -= 1
    seq_tab = list(np.maximum(np.cumsum(decode) - 1, 0))
    qi_of = jnp.asarray([qi for qi, _ in tiles], jnp.int32)
    ki_of = jnp.asarray([ki for _, ki in tiles], jnp.int32)
    seq_of = jnp.asarray(seq_tab, jnp.int32)
    dec_of = jnp.asarray(decode, jnp.int32)

    tok_map = lambda b, t, pt, qo, ko, so, do: (so[b * n_tiles + t], 0)

    def page_spec(j):
        return pl.BlockSpec((None, DIFF_QK_W, page),
                            lambda b, t, pt, qo, ko, so, do: (pt[so[b * n_tiles + t] * n_pages + j], 0, 0))

    const = lambda shape: pl.BlockSpec(shape, lambda b, t, *_: (0,) * len(shape))
    return pl.pallas_call(
        functools.partial(_attn_kernel, n_pages=n_pages, lam_init=lam_init),
        grid_spec=pltpu.PrefetchScalarGridSpec(
            num_scalar_prefetch=5,
            grid=(batch, n_tiles),
            in_specs=[
                const((4, DIFF_HEAD_DIM)),
                const((1, DIFF_V_DIM)),
                pl.BlockSpec((None, 2 * DIFF_QK_W, blk), lambda b, t, pt, qo, ko, so, do: (b, 0, qo[t])),
                pl.BlockSpec((blk, DIFF_QK_W), lambda b, t, pt, qo, ko, so, do: (b * nb + ko[t], 0)),
                pl.BlockSpec((None, DIFF_V_W, blk), lambda b, t, pt, qo, ko, so, do: (b, 0, ko[t])),
                pl.BlockSpec((dec_seq, DIFF_QK_W), tok_map),
                pl.BlockSpec((dec_seq, DIFF_QK_W), tok_map),
                pl.BlockSpec((dec_seq, DIFF_V_W), tok_map),
            ] + [page_spec(j) for j in range(n_pages)] * 2,
            out_specs=[
                pl.BlockSpec((blk, DIFF_V_W), lambda b, t, pt, qo, ko, so, do: (b * nb + qo[t], 0)),
                pl.BlockSpec((dec_seq, DIFF_V_W), tok_map),
            ],
            scratch_shapes=[
                pltpu.VMEM((n_stat, blk), F32),
                pltpu.VMEM((n_stat, ACC_ROWS, blk), F32),
                pltpu.VMEM((2, blk, blk), F32),
                pltpu.VMEM((2, 1, blk), F32),
            ],
        ),
        out_shape=[
            jax.ShapeDtypeStruct((batch * seq, DIFF_V_W), BF16),
            jax.ShapeDtypeStruct((dec_batch * dec_seq, DIFF_V_W), F32),
        ],
        compiler_params=_params("arbitrary", "arbitrary"),
        name="attention",
    )(page_table.reshape(-1), qi_of, ki_of, seq_of, dec_of, lams, subln, qx, kn, vt, qa_s, ka_s, va_s,
      *([kt_pages] * n_pages), *([v_pages] * n_pages))


def _sample_probs(h, q_ref, kn_ref, kt_pages):
    t = q_ref.shape[0]
    page = kt_pages[0].shape[1]
    hs = slice(h * DIFF_V_DIM, (h + 1) * DIFF_V_DIM)
    pad = jnp.zeros((page - t, DIFF_V_DIM), BF16)
    row = lax.broadcasted_iota(jnp.int32, (2 * t, page), 0) % t
    col = lax.broadcasted_iota(jnp.int32, (2 * t, page), 1)
    lane = lax.broadcasted_iota(jnp.int32, (t, DIFF_V_DIM), 1)
    qh = q_ref[:, hs].astype(BF16)
    zero = jnp.zeros_like(qh)
    q = jnp.concatenate([jnp.where(lane < DIFF_HEAD_DIM, qh, zero), jnp.where(lane >= DIFF_HEAD_DIM, qh, zero)], axis=0)
    kt = jnp.concatenate([kp[hs, :] for kp in kt_pages], axis=1).astype(BF16)
    s_cache = jnp.dot(q, kt, preferred_element_type=F32)
    k_new = jnp.concatenate([kn_ref[:, hs].astype(BF16), pad], axis=0)
    s_new = jnp.where(col <= row, _dot_nt(q, k_new), NEG_INF)
    s = jnp.concatenate([s_cache, s_new], axis=1)
    p = jnp.exp(s - jnp.max(s, axis=1, keepdims=True))
    return p.astype(BF16), jnp.sum(p, axis=1, keepdims=True)


def _sample_values(h0, probs, lam, sub_ref, vn_ref, v_pages, o_ref, lam_init):
    (p0, l0), (p1, l1) = probs
    t = vn_ref.shape[0]
    page = v_pages[0].shape[0] // DIFF_HEADS
    p = jnp.concatenate([p0, p1], axis=0)
    pad = jnp.zeros((page - t, 2 * DIFF_V_DIM), BF16)
    v_new = jnp.concatenate([vn_ref[:, h0 * DIFF_V_DIM:(h0 + 2) * DIFF_V_DIM].astype(BF16), pad], axis=0)
    v_cache = [
        jnp.concatenate([vp[pl.ds(h0 + d, page, stride=DIFF_HEADS), :] for d in (0, 1)], axis=1).astype(BF16)
        for vp in v_pages]
    acc = jnp.dot(p, jnp.concatenate(v_cache + [v_new], axis=0), preferred_element_type=F32)
    for d, l in ((0, l0), (1, l1)):
        o = acc[2 * t * d:2 * t * (d + 1), d * DIFF_V_DIM:(d + 1) * DIFF_V_DIM] / l
        hs = slice((h0 + d) * DIFF_V_DIM, (h0 + d + 1) * DIFF_V_DIM)
        o_ref[:, hs] = _sub_layer_norm(o[:t] - lam * o[t:], sub_ref, lam_init).astype(o_ref.dtype)


def _iota_f32(shape, axis):
    return lax.broadcasted_iota(jnp.int32, shape, axis).astype(F32)


def _ret_decays(c, dk, dv, lg):
    rel = _iota_f32((c, c), 0) - _iota_f32((c, c), 1)
    decay = jnp.where(rel >= 0, jnp.exp(lg * jnp.maximum(rel, 0.0)), 0.0)
    q_dec = jnp.exp((_iota_f32((c, dk), 0) + 1.0) * lg)
    k_dec = jnp.exp((c - 1.0 - _iota_f32((c, dk), 0)) * lg)
    s_dec = jnp.exp(jnp.full((1, dv), float(c), F32) * lg)
    return decay, q_dec, k_dec, s_dec


def _ret_chunk(s_prev, q, k, v, decays):
    decay, q_dec, k_dec, s_dec = decays
    qk = _dot_nt(q.astype(BF16), k.astype(BF16)) * decay
    o = jnp.dot(qk.astype(BF16), v, preferred_element_type=F32)
    o = o + jnp.dot((q * q_dec).astype(BF16), s_prev.astype(BF16), preferred_element_type=F32)
    kv = jnp.dot((k * k_dec).T.astype(BF16), v, preferred_element_type=F32)
    return s_dec * s_prev + kv, o


def _gated_ret_out(o, gr):
    return _rms(o) * (gr * _sigmoid(gr))


def _retention_kernel(lg_ref, q_ref, k_ref, v_ref, gr_ref, ss_ref, qs_ref, ks_ref, vs_ref, grs_ref, r_ref, state_ref,
                      rs_ref, sso_ref, s_ref):
    c = pl.program_id(1)

    @pl.when(c == 0)
    def _():
        s_ref[...] = jnp.zeros(s_ref.shape, F32)

    n_stages = 4

    def decode_heads(stage):
        t = qs_ref.shape[0]
        for h in range(stage, RET_HEADS, n_stages):
            ks = slice(h * RET_KEY_DIM, (h + 1) * RET_KEY_DIM)
            vs = slice(h * RET_VALUE_DIM, (h + 1) * RET_VALUE_DIM)
            decays = _ret_decays(t, RET_KEY_DIM, RET_VALUE_DIM, lg_ref[h])
            s_new, o = _ret_chunk(ss_ref[h], qs_ref[:, ks], ks_ref[:, ks], vs_ref[:, vs].astype(BF16), decays)
            sso_ref[h] = s_new
            rs_ref[:, vs] = _gated_ret_out(o, grs_ref[:, vs]).astype(rs_ref.dtype)

    decay, q_dec, k_dec, s_dec = _ret_decays(RET_CHUNK, RET_KEY_DIM, RET_VALUE_DIM, lg_ref[pl.program_id(0)])
    batch = range(q_ref.shape[0])
    qk = [(_dot_nt(q_ref[b], k_ref[b]) * decay).astype(BF16) for b in batch]
    o = [jnp.dot(qk[b], v_ref[b], preferred_element_type=F32) for b in batch]
    decode_heads(0)
    for b in batch:
        qd = (q_ref[b].astype(F32) * q_dec).astype(BF16)
        o[b] = o[b] + jnp.dot(qd, s_ref[b].astype(BF16), preferred_element_type=F32)
    decode_heads(1)
    for b in batch:
        kd = (k_ref[b].astype(F32) * k_dec).T.astype(BF16)
        s_ref[b] = s_dec * s_ref[b] + jnp.dot(kd, v_ref[b], preferred_element_type=F32)
    decode_heads(2)
    for b in batch:
        r_ref[b] = _gated_ret_out(o[b], gr_ref[b].astype(F32)).astype(r_ref.dtype)
    decode_heads(3)

    @pl.when(c == pl.num_programs(1) - 1)
    def _():
        state_ref[...] = s_ref[...]


def _retention(log_gamma, qr, kr, vr, gr, state_s, qr_s, kr_s, vr_s, gr_s, batch, seq, dec_seq):
    nc = seq // RET_CHUNK
    dec_batch = state_s.shape[0]
    assert RET_HEADS * nc >= dec_batch
    seq_of = lambda h, c: jnp.minimum(h * nc + c, dec_batch - 1)
    tok = lambda w: pl.BlockSpec((batch, RET_CHUNK, w), lambda h, c: (0, c, h))
    tok_s = lambda w: pl.BlockSpec((dec_seq, w), lambda h, c: (seq_of(h, c), 0))
    state_s_spec = pl.BlockSpec((None, RET_HEADS, RET_KEY_DIM, RET_VALUE_DIM), lambda h, c: (seq_of(h, c), 0, 0, 0))
    return pl.pallas_call(
        _retention_kernel,
        grid=(RET_HEADS, nc),
        in_specs=[pl.BlockSpec(memory_space=pltpu.SMEM), tok(RET_KEY_DIM), tok(RET_KEY_DIM), tok(RET_VALUE_DIM),
                  tok(RET_VALUE_DIM), state_s_spec, tok_s(RET_QK_W), tok_s(RET_QK_W), tok_s(RET_V_W), tok_s(RET_V_W)],
        out_specs=[
            tok(RET_VALUE_DIM),
            pl.BlockSpec((batch, None, RET_KEY_DIM, RET_VALUE_DIM), lambda h, c: (0, h, 0, 0)),
            tok_s(RET_V_W),
            state_s_spec,
        ],
        out_shape=[
            jax.ShapeDtypeStruct((batch, seq, RET_V_W), BF16),
            jax.ShapeDtypeStruct((batch, RET_HEADS, RET_KEY_DIM, RET_VALUE_DIM), F32),
            jax.ShapeDtypeStruct((dec_batch * dec_seq, RET_V_W), F32),
            jax.ShapeDtypeStruct(state_s.shape, F32),
        ],
        scratch_shapes=[pltpu.VMEM((batch, RET_KEY_DIM, RET_VALUE_DIM), F32)],
        compiler_params=_params("arbitrary", "arbitrary"),
        name="retention",
    )(log_gamma, qr, kr, vr, gr, state_s, qr_s, kr_s, vr_s, gr_s)


def _merge_ffn_kernel(h_ref, a_ref, r_ref, ga_ref, gb_ref, bg_ref, wpa_ref, wpb_ref, wo_ref, gffn_ref, wgu_ref,
                      wdn_ref, gfin_ref, y_ref):
    pa = jnp.dot(a_ref[...].astype(BF16), wpa_ref[...], preferred_element_type=F32)
    pb = jnp.dot(r_ref[...].astype(BF16), wpb_ref[...], preferred_element_type=F32)
    m = (_sigmoid(ga_ref[...] + bg_ref[:, :D_MODEL]) * pa + _sigmoid(gb_ref[...] + bg_ref[:, D_MODEL:]) * pb)
    h1 = h_ref[...] + jnp.dot(m.astype(BF16), wo_ref[...], preferred_element_type=F32)
    xn = (_rms(h1) * gffn_ref[...]).astype(BF16)
    gu = jnp.dot(xn, wgu_ref[...], preferred_element_type=F32)
    g, u = gu[:, :FFN_HIDDEN], gu[:, FFN_HIDDEN:]
    hid = (g * _sigmoid(g) * u).astype(BF16)
    h2 = h1 + jnp.dot(hid, wdn_ref[...], preferred_element_type=F32)
    y_ref[...] = _rms(h2) * gfin_ref[...]


def _merge_ffn(h2d, a, r, ga, gb, b_gate, w_pa, w_pb, w_o, g_ffn, w_gu, w_dn, g_fin, tm):
    rows = h2d.shape[0]
    row = lambda w: pl.BlockSpec((tm, w), lambda i: (i, 0))
    return pl.pallas_call(
        _merge_ffn_kernel,
        grid=(rows // tm,),
        in_specs=[
            row(D_MODEL), row(DIFF_V_W), row(RET_V_W), row(D_MODEL), row(D_MODEL),
            _resident((1, 2 * D_MODEL)),
            _resident((DIFF_V_W, D_MODEL)), _resident((RET_V_W, D_MODEL)), _resident((D_MODEL, D_MODEL)),
            _resident((1, D_MODEL)),
            _resident((D_MODEL, 2 * FFN_HIDDEN)), _resident((FFN_HIDDEN, D_MODEL)),
            _resident((1, D_MODEL)),
        ],
        out_specs=row(D_MODEL),
        out_shape=jax.ShapeDtypeStruct((rows, D_MODEL), F32),
        compiler_params=_params("parallel"),
        name="merge_ffn",
    )(h2d, a, r, ga, gb, b_gate.reshape(1, -1), w_pa, w_pb, w_o, g_ffn.reshape(1, -1), w_gu, w_dn,
      g_fin.reshape(1, -1))


def kernel(x_prompt, x_sample, cache_k, cache_v, state_ret, page_table, norm_mix_g, w_in, b_gate, lambda_q1, lambda_k1,
           lambda_q2, lambda_k2, subln_g, w_pa, w_pb, w_o, norm_ffn_g, w_gu, w_down, norm_final_g):
    batch, seq, _ = x_prompt.shape
    dec_batch, dec_seq, _ = x_sample.shape
    depth, n_phys, page = cache_k.shape[:3]
    assert depth == 1, "the final norm is fused into the (single) layer's FFN kernel"
    past = page_table.shape[1] * page
    tm = 256

    log_gamma = jnp.log(1.0 - jnp.exp2(-5.0 - jnp.arange(RET_HEADS, dtype=F32)))
    tab_p = jnp.asarray(_rotation_tables(np.arange(seq)))
    tab_s = jnp.asarray(np.tile(_rotation_tables(np.arange(dec_seq) + past), (tm // dec_seq, 1)))

    l = 0
    lam_init = _lam_init(l)
    lams = jnp.stack([lambda_q1[l], lambda_k1[l], lambda_q2[l], lambda_k2[l]]).astype(F32)
    subln = subln_g[l].reshape(1, DIFF_V_DIM).astype(F32)
    w_in_b = w_in[l].astype(BF16)
    weights = (b_gate[l], w_pa[l].astype(BF16), w_pb[l].astype(BF16), w_o[l].astype(BF16), norm_ffn_g[l],
               w_gu[l].astype(BF16), w_down[l].astype(BF16), norm_final_g)
    kt_pages = jnp.transpose(cache_k[l], (0, 2, 3, 4, 1)).reshape(n_phys, DIFF_QK_W, page)
    v_pages = cache_v[l].reshape(n_phys, page * DIFF_HEADS, DIFF_V_DIM)

    hp = x_prompt.reshape(batch * seq, D_MODEL)
    hs = x_sample.reshape(dec_batch * dec_seq, D_MODEL)
    qx, kn, kt_p, vt, v_p, qr_p, kr_p, vr_p, gr_p, ga_p, gb_p = _project_prompt(
        hp, norm_mix_g[l], w_in_b, tab_p, batch, seq, tm)
    qa_s, ka_s, va_s, qr_s, kr_s, vr_s, gr_s, ga_s, gb_s = _project_sample(hs, norm_mix_g[l], w_in_b, tab_s, tm)
    a_p, a_s = _attention(page_table, lams, subln, qx, kn, vt, qa_s, ka_s, va_s, kt_pages, v_pages, batch, seq,
                          dec_seq, lam_init)

    as_seq = lambda x: x.reshape(batch, seq, x.shape[-1])
    r_p, s_p, r_s, s_s = _retention(log_gamma, as_seq(qr_p), as_seq(kr_p), as_seq(vr_p), as_seq(gr_p), state_ret[l],
                                    qr_s, kr_s, vr_s, gr_s, batch, seq, dec_seq)
    y_p = _merge_ffn(hp, a_p, r_p.reshape(batch * seq, RET_V_W), ga_p, gb_p, *weights, tm)
    y_s = _merge_ffn(hs, a_s, r_s, ga_s, gb_s, *weights, tm)

    k_p = jnp.transpose(kt_p.reshape(1, batch, DIFF_HEADS, 2, DIFF_HEAD_DIM, seq), (0, 1, 5, 2, 3, 4))
    return (
        y_p.reshape(batch, seq, D_MODEL),
        y_s.reshape(dec_batch, dec_seq, D_MODEL),
        k_p,
        v_p.reshape(1, batch, seq, DIFF_HEADS, DIFF_V_DIM),
        s_p[None],
        ka_s.reshape(1, dec_batch, dec_seq, DIFF_HEADS, 2, DIFF_HEAD_DIM),
        va_s.reshape(1, dec_batch, dec_seq, DIFF_HEADS, DIFF_V_DIM),
        s_s[None],
    )
```

```python
import functools
import math

import jax
import jax.numpy as jnp
import numpy as np
from jax import lax
from jax.experimental import pallas as pl
from jax.experimental.pallas import tpu as pltpu

F32 = jnp.float32
BF16 = jnp.bfloat16

D_MODEL = 1024
DIFF_HEADS = 8
DIFF_HEAD_DIM = 64
DIFF_V_DIM = 2 * DIFF_HEAD_DIM
ROT_DIM = DIFF_HEAD_DIM // 4
ROPE_THETA = 500000.0
RET_HEADS = 4
RET_KEY_DIM = 256
RET_VALUE_DIM = 512
RET_THETA = 10000.0
RET_CHUNK = 128
FFN_HIDDEN = 2816
NORM_EPS = 1e-5
NEG_INF = -1e30

DIFF_QK_W = DIFF_HEADS * 2 * DIFF_HEAD_DIM
DIFF_V_W = DIFF_HEADS * DIFF_V_DIM
RET_QK_W = RET_HEADS * RET_KEY_DIM
RET_V_W = RET_HEADS * RET_VALUE_DIM
IN_W = 3 * DIFF_QK_W + 2 * RET_QK_W + 2 * RET_V_W + 2 * D_MODEL

LANES = 128
ROPE_TAB_W = 3 * LANES
RET_TAB_W = 3 * RET_KEY_DIM
TAB_W = ROPE_TAB_W + RET_TAB_W
VMEM_LIMIT = 56 * 1024 * 1024
BF16_SUBLANES = 16
ACC_ROWS = DIFF_V_DIM + BF16_SUBLANES
LOG2E = math.log2(math.e)


def _lam_init(layer):
    return 0.8 - 0.6 * math.exp(-0.3 * layer)


def _sigmoid(x):
    return 1.0 / (1.0 + jnp.exp(-x))


def _rms(x):
    return x * lax.rsqrt(jnp.mean(x * x, axis=-1, keepdims=True) + NORM_EPS)


def _resident(shape):
    return pl.BlockSpec(shape, lambda *_: (0,) * len(shape), pipeline_mode=pl.Buffered(1))


def _params(*semantics):
    return pltpu.CompilerParams(dimension_semantics=semantics, vmem_limit_bytes=VMEM_LIMIT)


def _rotation_tables(pos):
    pos = np.asarray(pos, np.float64)
    half = ROT_DIM // 2
    freqs = np.exp(-math.log(ROPE_THETA) * np.arange(half) * (2.0 / ROT_DIM))
    ang = pos[:, None] * freqs[None, :]
    cos, sin = np.cos(ang), np.sin(ang)
    lane = np.arange(LANES) % DIFF_HEAD_DIM
    idx = lane % half
    rope_c = np.where(lane < ROT_DIM, cos[:, idx], 1.0)
    rope_s1 = np.where(lane < half, -sin[:, idx], 0.0)
    rope_s2 = np.where((lane >= half) & (lane < ROT_DIM), sin[:, idx], 0.0)

    rhalf = RET_KEY_DIM // 2
    angle = np.exp(-math.log(RET_THETA) * np.linspace(0.0, 1.0, rhalf))
    rang = pos[:, None] * angle[None, :]
    rcos, rsin = np.cos(rang), np.sin(rang)
    lane2 = np.arange(RET_KEY_DIM)
    pair = lane2 // 2
    even = lane2 % 2 == 0
    ret_c = rcos[:, pair]
    ret_s1 = np.where(even, -rsin[:, pair], 0.0)
    ret_s2 = np.where(~even, rsin[:, pair], 0.0)
    return np.concatenate([rope_c, rope_s1, rope_s2, ret_c, ret_s1, ret_s2], axis=1).astype(np.float32)


def _partial_rope(acc, tab):
    c, s1, s2 = tab[:, 0:LANES], tab[:, LANES:2 * LANES], tab[:, 2 * LANES:3 * LANES]
    half = ROT_DIM // 2
    outs = []
    for g in range(acc.shape[1] // LANES):
        x = acc[:, g * LANES:(g + 1) * LANES]
        outs.append(x * c + pltpu.roll(x, LANES - half, 1) * s1 + pltpu.roll(x, half, 1) * s2)
    return jnp.concatenate(outs, axis=1)


def _ret_rotate(acc, tab):
    per_head = RET_KEY_DIM // LANES
    outs = []
    for g in range(acc.shape[1] // LANES):
        o = (g % per_head) * LANES
        c = tab[:, o:o + LANES]
        s1 = tab[:, RET_KEY_DIM + o:RET_KEY_DIM + o + LANES]
        s2 = tab[:, 2 * RET_KEY_DIM + o:2 * RET_KEY_DIM + o + LANES]
        x = acc[:, g * LANES:(g + 1) * LANES]
        outs.append(x * c + pltpu.roll(x, LANES - 1, 1) * s1 + pltpu.roll(x, 1, 1) * s2)
    return jnp.concatenate(outs, axis=1)


class _Projector:
    def __init__(self, x_ref, g_ref, w_ref, tab_ref, xn_ref):
        xn_ref[...] = (_rms(x_ref[...]) * g_ref[...]).astype(BF16)
        self._xn_ref, self._w_ref = xn_ref, w_ref
        self._rope_tab = tab_ref[:, 0:ROPE_TAB_W]
        self._ret_tab = tab_ref[:, ROPE_TAB_W:TAB_W]
        self._col = 0

    def _mm(self, n):
        c0, self._col = self._col, self._col + n
        return jnp.dot(self._xn_ref[...], self._w_ref[:, c0:c0 + n], preferred_element_type=F32)

    def qa(self):
        return _partial_rope(self._mm(DIFF_QK_W), self._rope_tab) * (DIFF_HEAD_DIM ** -0.5)

    def ka(self):
        return _partial_rope(self._mm(DIFF_QK_W), self._rope_tab)

    def va(self):
        return self._mm(DIFF_V_W)

    def qr(self):
        return _ret_rotate(self._mm(RET_QK_W), self._ret_tab)

    def kr(self):
        return _ret_rotate(self._mm(RET_QK_W), self._ret_tab) * (RET_KEY_DIM ** -0.5)

    def wide(self):
        return self._mm(RET_V_W)

    def gate(self):
        return self._mm(D_MODEL)


def _proj_sample_kernel(x_ref, g_ref, w_ref, tab_ref, qa_ref, ka_ref, va_ref, qr_ref, kr_ref, vr_ref, gr_ref, ga_ref,
                        gb_ref, xn_ref):
    p = _Projector(x_ref, g_ref, w_ref, tab_ref, xn_ref)
    qa_ref[...] = p.qa()
    ka_ref[...] = p.ka()
    va_ref[...] = p.va()
    qr_ref[...] = p.qr()
    kr_ref[...] = p.kr()
    vr_ref[...] = p.wide()
    gr_ref[...] = p.wide()
    ga_ref[...] = p.gate()
    gb_ref[...] = p.gate()


def _proj_prompt_kernel(x_ref, g_ref, w_ref, tab_ref, qx_ref, kn_ref, kt_ref, vt_ref, vo_ref, qr_ref, kr_ref, vr_ref,
                        gr_ref, ga_ref, gb_ref, xn_ref):
    p = _Projector(x_ref, g_ref, w_ref, tab_ref, xn_ref)
    tm = x_ref.shape[0]
    qt = (p.qa() * LOG2E).T.astype(BF16)
    zeros = jnp.zeros((DIFF_HEAD_DIM, tm), BF16)
    for hc in range(2 * DIFF_HEADS):
        comp = hc % 2
        src = qt[hc * DIFF_HEAD_DIM:(hc + 1) * DIFF_HEAD_DIM]
        lo, hi = (src, zeros) if comp == 0 else (zeros, src)
        qx_ref[hc * DIFF_V_DIM:hc * DIFF_V_DIM + DIFF_HEAD_DIM, :] = lo
        qx_ref[hc * DIFF_V_DIM + DIFF_HEAD_DIM:(hc + 1) * DIFF_V_DIM, :] = hi
    k = p.ka()
    kn_ref[...] = k.astype(BF16)
    kt_ref[...] = k.T
    v = p.va()
    vt_ref[...] = v.T.astype(BF16)
    for h in range(DIFF_HEADS):
        vo_ref[pl.ds(h, tm, stride=DIFF_HEADS), :] = v[:, h * DIFF_V_DIM:(h + 1) * DIFF_V_DIM]
    qr_ref[...] = p.qr().astype(BF16)
    kr_ref[...] = p.kr().astype(BF16)
    vr_ref[...] = p.wide().astype(BF16)
    gr_ref[...] = p.wide().astype(BF16)
    ga_ref[...] = p.gate()
    gb_ref[...] = p.gate()


def _proj_in_specs(tm, tab_blocks):
    return [
        pl.BlockSpec((tm, D_MODEL), lambda i: (i, 0)),
        _resident((1, D_MODEL)),
        _resident((D_MODEL, IN_W)),
        pl.BlockSpec((tm, TAB_W), lambda i: (i % tab_blocks, 0)),
    ]


def _project_sample(x2d, gain, w_bf16, tab, tm):
    rows = x2d.shape[0]
    widths = (DIFF_QK_W, DIFF_QK_W, DIFF_V_W, RET_QK_W, RET_QK_W, RET_V_W, RET_V_W, D_MODEL, D_MODEL)
    return pl.pallas_call(
        _proj_sample_kernel,
        grid=(rows // tm,),
        in_specs=_proj_in_specs(tm, tab.shape[0] // tm),
        out_specs=[pl.BlockSpec((tm, w), lambda i: (i, 0)) for w in widths],
        out_shape=[jax.ShapeDtypeStruct((rows, w), F32) for w in widths],
        scratch_shapes=[pltpu.VMEM((tm, D_MODEL), BF16)],
        compiler_params=_params("parallel"),
        name="proj_sample",
    )(x2d, gain.reshape(1, D_MODEL), w_bf16, tab)


def _project_prompt(x2d, gain, w_bf16, tab, batch, seq, tm):
    rows = batch * seq
    per_b = seq // tm
    row = lambda w: pl.BlockSpec((tm, w), lambda i: (i, 0))
    transposed = lambda h: pl.BlockSpec((None, h, tm), lambda i: (i // per_b, 0, i % per_b))
    specs = [
        (transposed(2 * DIFF_QK_W), (batch, 2 * DIFF_QK_W, seq), BF16),
        (row(DIFF_QK_W), (rows, DIFF_QK_W), BF16),
        (transposed(DIFF_QK_W), (batch, DIFF_QK_W, seq), F32),
        (transposed(DIFF_V_W), (batch, DIFF_V_W, seq), BF16),
        (pl.BlockSpec((tm * DIFF_HEADS, DIFF_V_DIM), lambda i: (i, 0)), (rows * DIFF_HEADS, DIFF_V_DIM), F32),
        (row(RET_QK_W), (rows, RET_QK_W), BF16),
        (row(RET_QK_W), (rows, RET_QK_W), BF16),
        (row(RET_V_W), (rows, RET_V_W), BF16),
        (row(RET_V_W), (rows, RET_V_W), BF16),
        (row(D_MODEL), (rows, D_MODEL), F32),
        (row(D_MODEL), (rows, D_MODEL), F32),
    ]
    return pl.pallas_call(
        _proj_prompt_kernel,
        grid=(rows // tm,),
        in_specs=_proj_in_specs(tm, tab.shape[0] // tm),
        out_specs=[s for s, _, _ in specs],
        out_shape=[jax.ShapeDtypeStruct(shape, dtype) for _, shape, dtype in specs],
        scratch_shapes=[pltpu.VMEM((tm, D_MODEL), BF16)],
        compiler_params=_params("parallel"),
        name="proj_prompt",
    )(x2d, gain.reshape(1, D_MODEL), w_bf16, tab)


def _lambda(lam_ref, lam_init):
    lv = lam_ref[...]
    s1 = jnp.sum(lv[0:1] * lv[1:2], axis=-1, keepdims=True)
    s2 = jnp.sum(lv[2:3] * lv[3:4], axis=-1, keepdims=True)
    return jnp.exp(s1) - jnp.exp(s2) + lam_init


def _dot_nt(a, b):
    return lax.dot_general(a, b, (((1,), (1,)), ((), ())), preferred_element_type=F32)


def _sub_layer_norm(o, sub_ref, lam_init):
    return _rms(o) * sub_ref[...] * (1.0 - lam_init)


def _attn_kernel(pt_ref, qi_ref, ki_ref, seq_ref, dec_ref, lam_ref, sub_ref, qx_ref, k_ref, vt_ref, qs_ref, kn_ref,
                 vn_ref, *rest, n_pages, lam_init):
    del pt_ref, seq_ref
    kt_pages = rest[:n_pages]
    v_pages = rest[n_pages:2 * n_pages]
    o_ref, os_ref, m_ref, acc_ref, st_ref, mx_ref = rest[2 * n_pages:]
    qi = qi_ref[pl.program_id(1)]
    ki = ki_ref[pl.program_id(1)]
    has_decode = dec_ref[pl.program_id(0) * pl.num_programs(1) + pl.program_id(1)] == 1
    n_stat = 2 * DIFF_HEADS
    tk = k_ref.shape[0]
    lam = _lambda(lam_ref, lam_init)

    @pl.when(ki == 0)
    def _():
        m_ref[...] = jnp.full(m_ref.shape, NEG_INF, F32)
        acc_ref[...] = jnp.zeros(acc_ref.shape, F32)

    half = tk // 2

    def causal(st):
        key = lax.broadcasted_iota(jnp.int32, st.shape, 0)
        qry = lax.broadcasted_iota(jnp.int32, st.shape, 1)
        return jnp.where(key <= qry, st, NEG_INF)

    def scores(i, on_diagonal):
        h = i // 2
        hs = slice(h * DIFF_V_DIM, (h + 1) * DIFF_V_DIM)
        qs = slice(i * DIFF_V_DIM, (i + 1) * DIFF_V_DIM)
        if not on_diagonal:
            st = jnp.dot(k_ref[:, hs], qx_ref[qs, :], preferred_element_type=F32)
            st_ref[i % 2] = st
            mx_ref[i % 2] = jnp.max(st, axis=0, keepdims=True)
            return
        top = causal(jnp.dot(k_ref[0:half, hs], qx_ref[qs, :], preferred_element_type=F32))
        low = causal(jnp.dot(k_ref[half:tk, hs], qx_ref[qs, half:tk], preferred_element_type=F32))
        st_ref[i % 2, 0:half, :] = top
        st_ref[i % 2, half:tk, half:tk] = low
        top_max = jnp.max(top, axis=0, keepdims=True)
        low_max = jnp.max(low, axis=0, keepdims=True)
        mx_ref[i % 2] = jnp.concatenate([top_max[:, 0:half], jnp.maximum(top_max[:, half:tk], low_max)], axis=1)

    def weighted_values(i, vt_h, m_new, on_diagonal):
        if not on_diagonal:
            p = jnp.exp2(st_ref[i % 2] - m_new).astype(BF16)
            return jnp.dot(vt_h, p, preferred_element_type=F32)
        p_top = jnp.exp2(st_ref[i % 2, 0:half, :] - m_new).astype(BF16)
        p_low = jnp.exp2(st_ref[i % 2, half:tk, half:tk] - m_new[:, half:tk]).astype(BF16)
        upd = jnp.dot(vt_h[:, 0:half], p_top, preferred_element_type=F32)
        low = jnp.dot(vt_h[:, half:tk], p_low, preferred_element_type=F32)
        return jnp.concatenate([upd[:, 0:half], upd[:, half:tk] + low], axis=1)

    def step(on_diagonal, with_decode):
        ones = jnp.ones((ACC_ROWS - DIFF_V_DIM, tk), BF16)
        probs = []
        scores(0, on_diagonal)
        for i in range(n_stat):
            if i + 1 < n_stat:
                scores(i + 1, on_diagonal)
            h = i // 2
            vt_h = jnp.concatenate([vt_ref[h * DIFF_V_DIM:(h + 1) * DIFF_V_DIM, :], ones], axis=0)
            m_prev = m_ref[i:i + 1, :]
            m_new = jnp.maximum(m_prev, mx_ref[i % 2])
            acc_ref[i] = jnp.exp2(m_prev - m_new) * acc_ref[i] + weighted_values(i, vt_h, m_new, on_diagonal)
            m_ref[i:i + 1, :] = m_new
            if with_decode and i % 2 == 1:
                probs.append(_sample_probs(h, qs_ref, kn_ref, kt_pages))
            if with_decode and i % 4 == 3:
                _sample_values(h - 1, probs[h - 1:h + 1], lam, sub_ref, vn_ref, v_pages, os_ref, lam_init)

    def finish_q_block():
        for h in range(DIFF_HEADS):
            o1, o2 = (acc_ref[i, 0:DIFF_V_DIM, :] / acc_ref[i, DIFF_V_DIM:DIFF_V_DIM + 1, :] for i in (2 * h, 2 * h + 1))
            ot = o1 - lam * o2
            o_ref[:, h * DIFF_V_DIM:(h + 1) * DIFF_V_DIM] = _sub_layer_norm(ot.T, sub_ref, lam_init).astype(o_ref.dtype)

    @pl.when(ki < qi)
    def _():
        step(False, True)

    @pl.when((ki == qi) & has_decode)
    def _():
        step(True, True)
        finish_q_block()

    @pl.when((ki == qi) & jnp.logical_not(has_decode))
    def _():
        step(True, False)
        finish_q_block()


def _attention(page_table, lams, subln, qx, kn, vt, qa_s, ka_s, va_s, kt_pages, v_pages, batch, seq, dec_seq, lam_init,
               blk=512):
    nb = seq // blk
    n_stat = 2 * DIFF_HEADS
    dec_batch, n_pages = page_table.shape
    page = kt_pages.shape[2]
    tiles = [(qi, ki) for qi in range(nb) for ki in range(qi + 1)]
    n_tiles = len(tiles)
    flat = [tile for _ in range(batch) for tile in tiles]
    surplus = len(flat) - dec_batch
    assert 0 <= surplus <= batch * nb, "need at least one tile per decode sequence, surplus only on diagonal tiles"
    decode = [True] * len(flat)
    for f in reversed(range(len(flat))):
        if surplus and flat[f][0] == flat[f][1]:
            decode[f] = False
            surplus -= 1
    seq_tab = list(np.maximum(np.cumsum(decode) - 1, 0))
    qi_of = jnp.asarray([qi for qi, _ in tiles], jnp.int32)
    ki_of = jnp.asarray([ki for _, ki in tiles], jnp.int32)
    seq_of = jnp.asarray(seq_tab, jnp.int32)
    dec_of = jnp.asarray(decode, jnp.int32)

    tok_map = lambda b, t, pt, qo, ko, so, do: (so[b * n_tiles + t], 0)

    def page_spec(j):
        return pl.BlockSpec((None, DIFF_QK_W, page),
                            lambda b, t, pt, qo, ko, so, do: (pt[so[b * n_tiles + t] * n_pages + j], 0, 0))

    const = lambda shape: pl.BlockSpec(shape, lambda b, t, *_: (0,) * len(shape))
    return pl.pallas_call(
        functools.partial(_attn_kernel, n_pages=n_pages, lam_init=lam_init),
        grid_spec=pltpu.PrefetchScalarGridSpec(
            num_scalar_prefetch=5,
            grid=(batch, n_tiles),
            in_specs=[
                const((4, DIFF_HEAD_DIM)),
                const((1, DIFF_V_DIM)),
                pl.BlockSpec((None, 2 * DIFF_QK_W, blk), lambda b, t, pt, qo, ko, so, do: (b, 0, qo[t])),
                pl.BlockSpec((blk, DIFF_QK_W), lambda b, t, pt, qo, ko, so, do: (b * nb + ko[t], 0)),
                pl.BlockSpec((None, DIFF_V_W, blk), lambda b, t, pt, qo, ko, so, do: (b, 0, ko[t])),
                pl.BlockSpec((dec_seq, DIFF_QK_W), tok_map),
                pl.BlockSpec((dec_seq, DIFF_QK_W), tok_map),
                pl.BlockSpec((dec_seq, DIFF_V_W), tok_map),
            ] + [page_spec(j) for j in range(n_pages)] * 2,
            out_specs=[
                pl.BlockSpec((blk, DIFF_V_W), lambda b, t, pt, qo, ko, so, do: (b * nb + qo[t], 0)),
                pl.BlockSpec((dec_seq, DIFF_V_W), tok_map),
            ],
            scratch_shapes=[
                pltpu.VMEM((n_stat, blk), F32),
                pltpu.VMEM((n_stat, ACC_ROWS, blk), F32),
                pltpu.VMEM((2, blk, blk), F32),
                pltpu.VMEM((2, 1, blk), F32),
            ],
        ),
        out_shape=[
            jax.ShapeDtypeStruct((batch * seq, DIFF_V_W), BF16),
            jax.ShapeDtypeStruct((dec_batch * dec_seq, DIFF_V_W), F32),
        ],
        compiler_params=_params("arbitrary", "arbitrary"),
        name="attention",
    )(page_table.reshape(-1), qi_of, ki_of, seq_of, dec_of, lams, subln, qx, kn, vt, qa_s, ka_s, va_s,
      *([kt_pages] * n_pages), *([v_pages] * n_pages))


def _sample_probs(h, q_ref, kn_ref, kt_pages):
    t = q_ref.shape[0]
    page = kt_pages[0].shape[1]
    hs = slice(h * DIFF_V_DIM, (h + 1) * DIFF_V_DIM)
    pad = jnp.zeros((page - t, DIFF_V_DIM), BF16)
    row = lax.broadcasted_iota(jnp.int32, (2 * t, page), 0) % t
    col = lax.broadcasted_iota(jnp.int32, (2 * t, page), 1)
    lane = lax.broadcasted_iota(jnp.int32, (t, DIFF_V_DIM), 1)
    qh = q_ref[:, hs].astype(BF16)
    zero = jnp.zeros_like(qh)
    q = jnp.concatenate([jnp.where(lane < DIFF_HEAD_DIM, qh, zero), jnp.where(lane >= DIFF_HEAD_DIM, qh, zero)], axis=0)
    kt = jnp.concatenate([kp[hs, :] for kp in kt_pages], axis=1).astype(BF16)
    s_cache = jnp.dot(q, kt, preferred_element_type=F32)
    k_new = jnp.concatenate([kn_ref[:, hs].astype(BF16), pad], axis=0)
    s_new = jnp.where(col <= row, _dot_nt(q, k_new), NEG_INF)
    s = jnp.concatenate([s_cache, s_new], axis=1)
    p = jnp.exp(s - jnp.max(s, axis=1, keepdims=True))
    return p.astype(BF16), jnp.sum(p, axis=1, keepdims=True)


def _sample_values(h0, probs, lam, sub_ref, vn_ref, v_pages, o_ref, lam_init):
    (p0, l0), (p1, l1) = probs
    t = vn_ref.shape[0]
    page = v_pages[0].shape[0] // DIFF_HEADS
    p = jnp.concatenate([p0, p1], axis=0)
    pad = jnp.zeros((page - t, 2 * DIFF_V_DIM), BF16)
    v_new = jnp.concatenate([vn_ref[:, h0 * DIFF_V_DIM:(h0 + 2) * DIFF_V_DIM].astype(BF16), pad], axis=0)
    v_cache = [
        jnp.concatenate([vp[pl.ds(h0 + d, page, stride=DIFF_HEADS), :] for d in (0, 1)], axis=1).astype(BF16)
        for vp in v_pages]
    acc = jnp.dot(p, jnp.concatenate(v_cache + [v_new], axis=0), preferred_element_type=F32)
    for d, l in ((0, l0), (1, l1)):
        o = acc[2 * t * d:2 * t * (d + 1), d * DIFF_V_DIM:(d + 1) * DIFF_V_DIM] / l
        hs = slice((h0 + d) * DIFF_V_DIM, (h0 + d + 1) * DIFF_V_DIM)
        o_ref[:, hs] = _sub_layer_norm(o[:t] - lam * o[t:], sub_ref, lam_init).astype(o_ref.dtype)


def _iota_f32(shape, axis):
    return lax.broadcasted_iota(jnp.int32, shape, axis).astype(F32)


def _ret_decays(c, dk, dv, lg):
    rel = _iota_f32((c, c), 0) - _iota_f32((c, c), 1)
    decay = jnp.where(rel >= 0, jnp.exp(lg * jnp.maximum(rel, 0.0)), 0.0)
    q_dec = jnp.exp((_iota_f32((c, dk), 0) + 1.0) * lg)
    k_dec = jnp.exp((c - 1.0 - _iota_f32((c, dk), 0)) * lg)
    s_dec = jnp.exp(jnp.full((1, dv), float(c), F32) * lg)
    return decay, q_dec, k_dec, s_dec


def _ret_chunk(s_prev, q, k, v, decays):
    decay, q_dec, k_dec, s_dec = decays
    qk = _dot_nt(q.astype(BF16), k.astype(BF16)) * decay
    o = jnp.dot(qk.astype(BF16), v, preferred_element_type=F32)
    o = o + jnp.dot((q * q_dec).astype(BF16), s_prev.astype(BF16), preferred_element_type=F32)
    kv = jnp.dot((k * k_dec).T.astype(BF16), v, preferred_element_type=F32)
    return s_dec * s_prev + kv, o


def _gated_ret_out(o, gr):
    return _rms(o) * (gr * _sigmoid(gr))


def _retention_kernel(lg_ref, q_ref, k_ref, v_ref, gr_ref, ss_ref, qs_ref, ks_ref, vs_ref, grs_ref, r_ref, state_ref,
                      rs_ref, sso_ref, s_ref):
    c = pl.program_id(1)

    @pl.when(c == 0)
    def _():
        s_ref[...] = jnp.zeros(s_ref.shape, F32)

    n_stages = 4

    def decode_heads(stage):
        t = qs_ref.shape[0]
        for h in range(stage, RET_HEADS, n_stages):
            ks = slice(h * RET_KEY_DIM, (h + 1) * RET_KEY_DIM)
            vs = slice(h * RET_VALUE_DIM, (h + 1) * RET_VALUE_DIM)
            decays = _ret_decays(t, RET_KEY_DIM, RET_VALUE_DIM, lg_ref[h])
            s_new, o = _ret_chunk(ss_ref[h], qs_ref[:, ks], ks_ref[:, ks], vs_ref[:, vs].astype(BF16), decays)
            sso_ref[h] = s_new
            rs_ref[:, vs] = _gated_ret_out(o, grs_ref[:, vs]).astype(rs_ref.dtype)

    decay, q_dec, k_dec, s_dec = _ret_decays(RET_CHUNK, RET_KEY_DIM, RET_VALUE_DIM, lg_ref[pl.program_id(0)])
    batch = range(q_ref.shape[0])
    qk = [(_dot_nt(q_ref[b], k_ref[b]) * decay).astype(BF16) for b in batch]
    o = [jnp.dot(qk[b], v_ref[b], preferred_element_type=F32) for b in batch]
    decode_heads(0)
    for b in batch:
        qd = (q_ref[b].astype(F32) * q_dec).astype(BF16)
        o[b] = o[b] + jnp.dot(qd, s_ref[b].astype(BF16), preferred_element_type=F32)
    decode_heads(1)
    for b in batch:
        kd = (k_ref[b].astype(F32) * k_dec).T.astype(BF16)
        s_ref[b] = s_dec * s_ref[b] + jnp.dot(kd, v_ref[b], preferred_element_type=F32)
    decode_heads(2)
    for b in batch:
        r_ref[b] = _gated_ret_out(o[b], gr_ref[b].astype(F32)).astype(r_ref.dtype)
    decode_heads(3)

    @pl.when(c == pl.num_programs(1) - 1)
    def _():
        state_ref[...] = s_ref[...]


def _retention(log_gamma, qr, kr, vr, gr, state_s, qr_s, kr_s, vr_s, gr_s, batch, seq, dec_seq):
    nc = seq // RET_CHUNK
    dec_batch = state_s.shape[0]
    assert RET_HEADS * nc >= dec_batch
    seq_of = lambda h, c: jnp.minimum(h * nc + c, dec_batch - 1)
    tok = lambda w: pl.BlockSpec((batch, RET_CHUNK, w), lambda h, c: (0, c, h))
    tok_s = lambda w: pl.BlockSpec((dec_seq, w), lambda h, c: (seq_of(h, c), 0))
    state_s_spec = pl.BlockSpec((None, RET_HEADS, RET_KEY_DIM, RET_VALUE_DIM), lambda h, c: (seq_of(h, c), 0, 0, 0))
    return pl.pallas_call(
        _retention_kernel,
        grid=(RET_HEADS, nc),
        in_specs=[pl.BlockSpec(memory_space=pltpu.SMEM), tok(RET_KEY_DIM), tok(RET_KEY_DIM), tok(RET_VALUE_DIM),
                  tok(RET_VALUE_DIM), state_s_spec, tok_s(RET_QK_W), tok_s(RET_QK_W), tok_s(RET_V_W), tok_s(RET_V_W)],
        out_specs=[
            tok(RET_VALUE_DIM),
            pl.BlockSpec((batch, None, RET_KEY_DIM, RET_VALUE_DIM), lambda h, c: (0, h, 0, 0)),
            tok_s(RET_V_W),
            state_s_spec,
        ],
        out_shape=[
            jax.ShapeDtypeStruct((batch, seq, RET_V_W), BF16),
            jax.ShapeDtypeStruct((batch, RET_HEADS, RET_KEY_DIM, RET_VALUE_DIM), F32),
            jax.ShapeDtypeStruct((dec_batch * dec_seq, RET_V_W), F32),
            jax.ShapeDtypeStruct(state_s.shape, F32),
        ],
        scratch_shapes=[pltpu.VMEM((batch, RET_KEY_DIM, RET_VALUE_DIM), F32)],
        compiler_params=_params("arbitrary", "arbitrary"),
        name="retention",
    )(log_gamma, qr, kr, vr, gr, state_s, qr_s, kr_s, vr_s, gr_s)


def _merge_ffn_kernel(h_ref, a_ref, r_ref, ga_ref, gb_ref, bg_ref, wpa_ref, wpb_ref, wo_ref, gffn_ref, wgu_ref,
                      wdn_ref, gfin_ref, y_ref):
    pa = jnp.dot(a_ref[...].astype(BF16), wpa_ref[...], preferred_element_type=F32)
    pb = jnp.dot(r_ref[...].astype(BF16), wpb_ref[...], preferred_element_type=F32)
    m = (_sigmoid(ga_ref[...] + bg_ref[:, :D_MODEL]) * pa + _sigmoid(gb_ref[...] + bg_ref[:, D_MODEL:]) * pb)
    h1 = h_ref[...] + jnp.dot(m.astype(BF16), wo_ref[...], preferred_element_type=F32)
    xn = (_rms(h1) * gffn_ref[...]).astype(BF16)
    h2 = h1
    split = (FFN_HIDDEN // 512 + 1) * 256
    for c0, c1 in ((0, split), (split, FFN_HIDDEN)):
        g = jnp.dot(xn, wgu_ref[:, c0:c1], preferred_element_type=F32)
        u = jnp.dot(xn, wgu_ref[:, FFN_HIDDEN + c0:FFN_HIDDEN + c1], preferred_element_type=F32)
        hid = (g * _sigmoid(g) * u).astype(BF16)
        h2 = h2 + jnp.dot(hid, wdn_ref[c0:c1, :], preferred_element_type=F32)
    y_ref[...] = _rms(h2) * gfin_ref[...]


def _merge_ffn(h2d, a, r, ga, gb, b_gate, w_pa, w_pb, w_o, g_ffn, w_gu, w_dn, g_fin, tm):
    rows = h2d.shape[0]
    row = lambda w: pl.BlockSpec((tm, w), lambda i: (i, 0))
    return pl.pallas_call(
        _merge_ffn_kernel,
        grid=(rows // tm,),
        in_specs=[
            row(D_MODEL), row(DIFF_V_W), row(RET_V_W), row(D_MODEL), row(D_MODEL),
            _resident((1, 2 * D_MODEL)),
            _resident((DIFF_V_W, D_MODEL)), _resident((RET_V_W, D_MODEL)), _resident((D_MODEL, D_MODEL)),
            _resident((1, D_MODEL)),
            _resident((D_MODEL, 2 * FFN_HIDDEN)), _resident((FFN_HIDDEN, D_MODEL)),
            _resident((1, D_MODEL)),
        ],
        out_specs=row(D_MODEL),
        out_shape=jax.ShapeDtypeStruct((rows, D_MODEL), F32),
        compiler_params=_params("parallel"),
        name="merge_ffn",
    )(h2d, a, r, ga, gb, b_gate.reshape(1, -1), w_pa, w_pb, w_o, g_ffn.reshape(1, -1), w_gu, w_dn,
      g_fin.reshape(1, -1))


def kernel(x_prompt, x_sample, cache_k, cache_v, state_ret, page_table, norm_mix_g, w_in, b_gate, lambda_q1, lambda_k1,
           lambda_q2, lambda_k2, subln_g, w_pa, w_pb, w_o, norm_ffn_g, w_gu, w_down, norm_final_g):
    batch, seq, _ = x_prompt.shape
    dec_batch, dec_seq, _ = x_sample.shape
    depth, n_phys, page = cache_k.shape[:3]
    assert depth == 1, "the final norm is fused into the (single) layer's FFN kernel"
    past = page_table.shape[1] * page
    tm = 256

    log_gamma = jnp.log(1.0 - jnp.exp2(-5.0 - jnp.arange(RET_HEADS, dtype=F32)))
    tab_p = jnp.asarray(_rotation_tables(np.arange(seq)))
    tab_s = jnp.asarray(np.tile(_rotation_tables(np.arange(dec_seq) + past), (tm // dec_seq, 1)))

    l = 0
    lam_init = _lam_init(l)
    lams = jnp.stack([lambda_q1[l], lambda_k1[l], lambda_q2[l], lambda_k2[l]]).astype(F32)
    subln = subln_g[l].reshape(1, DIFF_V_DIM).astype(F32)
    w_in_b = w_in[l].astype(BF16)
    weights = (b_gate[l], w_pa[l].astype(BF16), w_pb[l].astype(BF16), w_o[l].astype(BF16), norm_ffn_g[l],
               w_gu[l].astype(BF16), w_down[l].astype(BF16), norm_final_g)
    kt_pages = jnp.transpose(cache_k[l], (0, 2, 3, 4, 1)).reshape(n_phys, DIFF_QK_W, page)
    v_pages = cache_v[l].reshape(n_phys, page * DIFF_HEADS, DIFF_V_DIM)

    hp = x_prompt.reshape(batch * seq, D_MODEL)
    hs = x_sample.reshape(dec_batch * dec_seq, D_MODEL)
    qx, kn, kt_p, vt, v_p, qr_p, kr_p, vr_p, gr_p, ga_p, gb_p = _project_prompt(
        hp, norm_mix_g[l], w_in_b, tab_p, batch, seq, tm)
    qa_s, ka_s, va_s, qr_s, kr_s, vr_s, gr_s, ga_s, gb_s = _project_sample(hs, norm_mix_g[l], w_in_b, tab_s, tm)
    a_p, a_s = _attention(page_table, lams, subln, qx, kn, vt, qa_s, ka_s, va_s, kt_pages, v_pages, batch, seq,
                          dec_seq, lam_init)

    as_seq = lambda x: x.reshape(batch, seq, x.shape[-1])
    r_p, s_p, r_s, s_s = _retention(log_gamma, as_seq(qr_p), as_seq(kr_p), as_seq(vr_p), as_seq(gr_p), state_ret[l],
                                    qr_s, kr_s, vr_s, gr_s, batch, seq, dec_seq)
    y_p = _merge_ffn(hp, a_p, r_p.reshape(batch * seq, RET_V_W), ga_p, gb_p, *weights, tm)
    y_s = _merge_ffn(hs, a_s, r_s, ga_s, gb_s, *weights, tm)

    k_p = jnp.transpose(kt_p.reshape(1, batch, DIFF_HEADS, 2, DIFF_HEAD_DIM, seq), (0, 1, 5, 2, 3, 4))
    return (
        y_p.reshape(batch, seq, D_MODEL),
        y_s.reshape(dec_batch, dec_seq, D_MODEL),
        k_p,
        v_p.reshape(1, batch, seq, DIFF_HEADS, DIFF_V_DIM),
        s_p[None],
        ka_s.reshape(1, dec_batch, dec_seq, DIFF_HEADS, 2, DIFF_HEAD_DIM),
        va_s.reshape(1, dec_batch, dec_seq, DIFF_HEADS, DIFF_V_DIM),
        s_s[None],
    )
```
